```python
import math
import jax, jax.numpy as jnp
from jax import lax
import numpy as np

D_MODEL = 4096
BATCH = 4
SEQ = 2048
DEPTH = 2
DEC_BATCH = 128
DEC_SEQ = 1
PAST_LEN = 16384
PAGE_SIZE = 128

W_BRANCH = D_MODEL // 2
N_BRANCH = 4
CONV_A_WIDTH = 31
S5_GROUP = 16
S5_GROUPS = W_BRANCH // S5_GROUP
S5_STATE = 64
SSD_HEADDIM = 64
SSD_HEADS = W_BRANCH // SSD_HEADDIM
SSD_GROUPS = 8
SSD_HPG = SSD_HEADS // SSD_GROUPS
SSD_STATE = 128
SSD_CONV = 4
SSD_CHUNK = 128
SSD_CONV_DIM = W_BRANCH + 2 * SSD_GROUPS * SSD_STATE
MLP_CHUNK = 128
MLP_GROUPS = 16
MLP_GROUP_DIM = W_BRANCH // MLP_GROUPS
IN_TOTAL = 9 * W_BRANCH + SSD_CONV_DIM + SSD_HEADS + N_BRANCH * D_MODEL
EPS = 1e-6

kernel_name = "hybrid_conv_s5_ssd_chunkmlp_decode_step"


def _in_proj_split_points():
    sizes = (2 * W_BRANCH, W_BRANCH,
             W_BRANCH, W_BRANCH,
             W_BRANCH, SSD_CONV_DIM, SSD_HEADS,
             W_BRANCH, W_BRANCH, W_BRANCH,
             N_BRANCH * D_MODEL)
    return [int(s) for s in np.cumsum(sizes)[:-1]]


def rms_norm(x, g):
    xf = x.astype(jnp.float32)
    y = xf * lax.rsqrt(jnp.mean(xf * xf, axis=-1, keepdims=True) + EPS)
    return (y * g.astype(jnp.float32)).astype(x.dtype)


def layer_norm(x, g, b):
    xf = x.astype(jnp.float32)
    xc = xf - jnp.mean(xf, axis=-1, keepdims=True)
    y = xc * lax.rsqrt(jnp.mean(xc * xc, axis=-1, keepdims=True) + EPS)
    return (y * g.astype(jnp.float32) + b.astype(jnp.float32)).astype(x.dtype)


def causal_dwconv(x_ext, w, b):
    c = w.shape[1]
    out = lax.conv_general_dilated(x_ext, w[:, None, :].astype(x_ext.dtype), window_strides=(1,),
                                   padding='VALID', dimension_numbers=('NWC', 'WIO', 'NWC'),
                                   feature_group_count=c)
    return out + b


def s5_scan(u, h0_re, h0_im, a_re, a_im, log_step, b_re, b_im, c_re, c_im, d_skip):
    f32 = jnp.float32
    uf = u.astype(f32)
    step = jnp.exp(log_step.astype(f32))[:, None]
    lam_re, lam_im = a_re.astype(f32), a_im.astype(f32)
    mag = jnp.exp(lam_re * step)
    ab_re = mag * jnp.cos(lam_im * step)
    ab_im = mag * jnp.sin(lam_im * step)
    den = lam_re * lam_re + lam_im * lam_im
    nr = ab_re - 1.0
    coef_re = (nr * lam_re + ab_im * lam_im) / den
    coef_im = (ab_im * lam_re - nr * lam_im) / den
    br, bi = b_re.astype(f32), b_im.astype(f32)
    bb_re = coef_re[..., None] * br - coef_im[..., None] * bi
    bb_im = coef_re[..., None] * bi + coef_im[..., None] * br
    bu_re = jnp.einsum('nlgh,gph->nlgp', uf, bb_re)
    bu_im = jnp.einsum('nlgh,gph->nlgp', uf, bb_im)
    hr0, hi0 = h0_re.astype(f32), h0_im.astype(f32)
    bu_re = bu_re.at[:, 0].add(ab_re * hr0 - ab_im * hi0)
    bu_im = bu_im.at[:, 0].add(ab_re * hi0 + ab_im * hr0)
    l = u.shape[1]
    a_re_l = jnp.broadcast_to(ab_re, (1, l) + ab_re.shape)
    a_im_l = jnp.broadcast_to(ab_im, (1, l) + ab_im.shape)

    def combine(e1, e2):
        a1r, a1i, b1r, b1i = e1
        a2r, a2i, b2r, b2i = e2
        return (a2r * a1r - a2i * a1i, a2r * a1i + a2i * a1r,
                a2r * b1r - a2i * b1i + b2r, a2r * b1i + a2i * b1r + b2i)

    _, _, hr, hi = lax.associative_scan(combine, (a_re_l, a_im_l, bu_re, bu_im), axis=1)
    y = (jnp.einsum('nlgp,ghp->nlgh', hr, c_re.astype(f32))
         - jnp.einsum('nlgp,ghp->nlgh', hi, c_im.astype(f32))
         + d_skip.astype(f32) * uf)
    return y, hr[:, -1], hi[:, -1]


def ssd_chunked(x, dt, a, bm, cm, h0):
    n, l = x.shape[0], x.shape[1]
    q = min(SSD_CHUNK, l)
    pad = (-l) % q
    padf = lambda t: jnp.pad(t, [(0, 0), (0, pad)] + [(0, 0)] * (t.ndim - 2))
    x, dt, bm, cm = padf(x), padf(dt), padf(bm), padf(cm)
    nc = (l + pad) // q
    xdt = (x * dt[..., None]).reshape(n, nc, q, SSD_GROUPS, SSD_HPG, SSD_HEADDIM)
    da = (dt * a).reshape(n, nc, q, SSD_GROUPS, SSD_HPG)
    bm = bm.reshape(n, nc, q, SSD_GROUPS, SSD_STATE)
    cm = cm.reshape(n, nc, q, SSD_GROUPS, SSD_STATE)
    da_cs = jnp.cumsum(da, axis=2)
    seg = da_cs[:, :, :, None] - da_cs[:, :, None, :]
    causal = jnp.tril(jnp.ones((q, q), dtype=bool))[:, :, None, None]
    decay = jnp.exp(jnp.where(causal, seg, -jnp.inf))
    cb = jnp.einsum('nclgk,ncmgk->nclmg', cm, bm)
    y_diag = jnp.einsum('nclmgr,ncmgrp->nclgrp', cb[..., None] * decay, xdt)
    decay_to_end = jnp.exp(da_cs[:, :, -1:] - da_cs)
    st = jnp.einsum('ncmgk,ncmgrp->ncgrpk', bm, xdt * decay_to_end[..., None])
    chunk_decay = jnp.exp(da_cs[:, :, -1])

    def step(h, inp):
        st_c, dec_c = inp
        return h * dec_c[..., None, None] + st_c, h

    h_final, h_in = lax.scan(step, h0, (jnp.swapaxes(st, 0, 1), jnp.swapaxes(chunk_decay, 0, 1)))
    h_in = jnp.swapaxes(h_in, 0, 1)
    y_off = jnp.einsum('nclgk,ncgrpk->nclgrp', cm, h_in) * jnp.exp(da_cs)[..., None]
    y = (y_diag + y_off).reshape(n, nc * q, SSD_GROUPS, SSD_HPG, SSD_HEADDIM)[:, :l]
    return y, h_final


def trunk_layer(x, conv_a_buf, s5_h_re, s5_h_im, ssd_h, ssd_buf,
                norm_pre_g, w_in, conv_a_w, conv_a_b, ln_a_g, ln_a_b,
                s5_a_re, s5_a_im, s5_log_step, s5_b_re, s5_b_im, s5_c_re, s5_c_im, s5_d,
                s5_w_glu, s5_b_glu,
                ssd_conv_w, ssd_conv_b, ssd_dt_bias, ssd_a_log, ssd_d, ssd_norm_g,
                mlp_ln_g, mlp_ln_b, mlp_w_s, mlp_b_s,
                w_branch, w_out, norm_post_g):
    f32 = jnp.float32
    n, l, _ = x.shape
    xn = rms_norm(x, norm_pre_g)
    proj = xn @ w_in
    (a_glu, a_gate, b_u, b_gate, c_z, c_xbc, c_dt,
     d_u, d_v, d_gate, merge) = jnp.split(proj, _in_proj_split_points(), axis=-1)

    a_val, a_g = jnp.split(a_glu, 2, axis=-1)
    h_a = a_val * jax.nn.sigmoid(a_g)
    ext_a = jnp.concatenate([conv_a_buf.astype(h_a.dtype), h_a], axis=1)
    new_conv_a = ext_a[:, -(CONV_A_WIDTH - 1):]
    h_a = jax.nn.silu(layer_norm(causal_dwconv(ext_a, conv_a_w, conv_a_b), ln_a_g, ln_a_b))
    out_a = h_a * jax.nn.silu(a_gate)

    y_b, new_s5_re, new_s5_im = s5_scan(b_u.reshape(n, l, S5_GROUPS, S5_GROUP), s5_h_re, s5_h_im,
                                        s5_a_re, s5_a_im, s5_log_step, s5_b_re, s5_b_im,
                                        s5_c_re, s5_c_im, s5_d)
    y_b = jax.nn.gelu(y_b.reshape(n, l, W_BRANCH))
    y_b = y_b * jax.nn.sigmoid(y_b @ s5_w_glu.astype(f32) + s5_b_glu.astype(f32))
    out_b = y_b * jax.nn.silu(b_gate.astype(f32))

    ext_c = jnp.concatenate([ssd_buf.astype(c_xbc.dtype), c_xbc], axis=1)
    new_conv_ssd = ext_c[:, -(SSD_CONV - 1):]
    xbc = jax.nn.silu(causal_dwconv(ext_c, ssd_conv_w, ssd_conv_b))
    xs, bm, cm = jnp.split(xbc, [W_BRANCH, W_BRANCH + SSD_GROUPS * SSD_STATE], axis=-1)
    xs = xs.reshape(n, l, SSD_GROUPS, SSD_HPG, SSD_HEADDIM).astype(f32)
    bm = bm.reshape(n, l, SSD_GROUPS, SSD_STATE).astype(f32)
    cm = cm.reshape(n, l, SSD_GROUPS, SSD_STATE).astype(f32)
    dt = jax.nn.softplus(c_dt.astype(f32) + ssd_dt_bias.astype(f32)).reshape(n, l, SSD_GROUPS, SSD_HPG)
    a_c = -jnp.exp(ssd_a_log.astype(f32)).reshape(SSD_GROUPS, SSD_HPG)
    y_c, new_ssd = ssd_chunked(xs, dt, a_c, bm, cm, ssd_h.astype(f32))
    y_c = y_c + ssd_d.astype(f32).reshape(SSD_GROUPS, SSD_HPG, 1) * xs
    y_c = y_c.reshape(n, l, W_BRANCH) * jax.nn.silu(c_z.astype(f32))
    out_c = rms_norm(y_c.reshape(n, l, SSD_GROUPS, W_BRANCH // SSD_GROUPS),
                     ssd_norm_g.reshape(SSD_GROUPS, W_BRANCH // SSD_GROUPS)).reshape(n, l, W_BRANCH)

    u_d = jax.nn.gelu(d_u)
    v_d = layer_norm(jax.nn.gelu(d_v), mlp_ln_g, mlp_ln_b)
    pad = (-l) % MLP_CHUNK
    nc = (l + pad) // MLP_CHUNK
    vp = jnp.pad(v_d, [(0, 0), (0, pad), (0, 0)]).reshape(n, nc, MLP_CHUNK, MLP_GROUPS, MLP_GROUP_DIM)
    ws = mlp_w_s * jnp.tril(jnp.ones((MLP_CHUNK, MLP_CHUNK), mlp_w_s.dtype))
    mixed = jnp.einsum('gts,ncsgd->nctgd', ws, vp) + mlp_b_s.T[None, None, :, :, None]
    mixed = mixed.reshape(n, nc * MLP_CHUNK, W_BRANCH)[:, :l]
    out_d = u_d * mixed * jax.nn.silu(d_gate)

    branches = jnp.stack([out_a.astype(f32), out_b, out_c.astype(f32), out_d.astype(f32)], axis=2)
    p = jnp.einsum('nlbw,bwd->nlbd', branches.astype(w_branch.dtype), w_branch)
    gates = jax.nn.sigmoid(merge.reshape(n, l, N_BRANCH, D_MODEL))
    merged = jnp.sum(gates * p, axis=2)
    y = x + rms_norm(merged @ w_out, norm_post_g)
    return y, new_conv_a, new_s5_re, new_s5_im, new_ssd, new_conv_ssd, v_d


def setup_inputs(seed: int = 0) -> dict:
    key = jax.random.key(seed)
    keys = jax.random.split(key, 64)
    ctr = [0]
    f32 = jnp.float32

    def nxt():
        k = keys[ctr[0]]
        ctr[0] += 1
        return k

    def nrm(shape, scale):
        return scale * jax.random.normal(nxt(), shape, f32)

    def unif(shape, lo, hi):
        return jax.random.uniform(nxt(), shape, f32, lo, hi)

    L, W = DEPTH, W_BRANCH
    inp = {}
    inp['x_prompt'] = nrm((BATCH, SEQ, D_MODEL), 1.0)
    inp['x_sample'] = nrm((DEC_BATCH, DEC_SEQ, D_MODEL), 1.0)
    inp['cache_conv_a'] = nrm((L, DEC_BATCH, CONV_A_WIDTH - 1, W), 0.5)
    inp['state_s5_re'] = nrm((L, DEC_BATCH, S5_GROUPS, S5_STATE), 0.5)
    inp['state_s5_im'] = nrm((L, DEC_BATCH, S5_GROUPS, S5_STATE), 0.5)
    inp['state_ssd'] = nrm((L, DEC_BATCH, SSD_GROUPS, SSD_HPG, SSD_HEADDIM, SSD_STATE), 0.1)
    inp['cache_conv_ssd'] = nrm((L, DEC_BATCH, SSD_CONV - 1, SSD_CONV_DIM), 1.0)
    inp['norm_pre_g'] = 1.0 + nrm((L, D_MODEL), 0.02)
    inp['w_in'] = nrm((L, D_MODEL, IN_TOTAL), D_MODEL ** -0.5)
    inp['conv_a_w'] = nrm((L, CONV_A_WIDTH, W), CONV_A_WIDTH ** -0.5)
    inp['conv_a_b'] = nrm((L, W), 0.02)
    inp['ln_a_g'] = 1.0 + nrm((L, W), 0.02)
    inp['ln_a_b'] = nrm((L, W), 0.02)
    inp['s5_a_re'] = -0.5 + nrm((L, S5_GROUPS, S5_STATE), 0.01)
    inp['s5_a_im'] = math.pi * jnp.arange(S5_STATE, dtype=f32) + nrm((L, S5_GROUPS, S5_STATE), 0.01)
    inp['s5_log_step'] = unif((L, S5_GROUPS), math.log(1e-3), math.log(1e-1))
    inp['s5_b_re'] = nrm((L, S5_GROUPS, S5_STATE, S5_GROUP), S5_GROUP ** -0.5)
    inp['s5_b_im'] = nrm((L, S5_GROUPS, S5_STATE, S5_GROUP), S5_GROUP ** -0.5)
    inp['s5_c_re'] = nrm((L, S5_GROUPS, S5_GROUP, S5_STATE), (2 * S5_STATE) ** -0.5)
    inp['s5_c_im'] = nrm((L, S5_GROUPS, S5_GROUP, S5_STATE), (2 * S5_STATE) ** -0.5)
    inp['s5_d'] = nrm((L, S5_GROUPS, S5_GROUP), 1.0)
    inp['s5_w_glu'] = nrm((L, W, W), W ** -0.5)
    inp['s5_b_glu'] = nrm((L, W), 0.02)
    inp['ssd_conv_w'] = nrm((L, SSD_CONV, SSD_CONV_DIM), SSD_CONV ** -0.5)
    inp['ssd_conv_b'] = nrm((L, SSD_CONV_DIM), 0.02)
    dt0 = jnp.exp(unif((L, SSD_HEADS), math.log(1e-3), math.log(1e-1)))
    inp['ssd_dt_bias'] = dt0 + jnp.log(-jnp.expm1(-dt0))
    inp['ssd_a_log'] = jnp.log(unif((L, SSD_HEADS), 1.0, 16.0))
    inp['ssd_d'] = 1.0 + nrm((L, SSD_HEADS), 0.02)
    inp['ssd_norm_g'] = 1.0 + nrm((L, W), 0.02)
    inp['mlp_ln_g'] = 1.0 + nrm((L, W), 0.02)
    inp['mlp_ln_b'] = nrm((L, W), 0.02)
    inp['mlp_w_s'] = nrm((L, MLP_GROUPS, MLP_CHUNK, MLP_CHUNK), MLP_CHUNK ** -0.5)
    inp['mlp_b_s'] = 1.0 + nrm((L, MLP_GROUPS, MLP_CHUNK), 0.02)
    inp['w_branch'] = nrm((L, N_BRANCH, W, D_MODEL), W ** -0.5)
    inp['w_out'] = nrm((L, D_MODEL, D_MODEL), D_MODEL ** -0.5)
    inp['norm_post_g'] = 1.0 + nrm((L, D_MODEL), 0.02)
    return inp


def reference(x_prompt, x_sample, cache_conv_a, state_s5_re, state_s5_im, state_ssd, cache_conv_ssd,
              norm_pre_g, w_in, conv_a_w, conv_a_b, ln_a_g, ln_a_b,
              s5_a_re, s5_a_im, s5_log_step, s5_b_re, s5_b_im, s5_c_re, s5_c_im, s5_d,
              s5_w_glu, s5_b_glu,
              ssd_conv_w, ssd_conv_b, ssd_dt_bias, ssd_a_log, ssd_d, ssd_norm_g,
              mlp_ln_g, mlp_ln_b, mlp_w_s, mlp_b_s,
              w_branch, w_out, norm_post_g):
    nb = x_prompt.shape[0]
    hp, hs = x_prompt, x_sample
    ca_p, ca_s, sre_p, sre_s, sim_p, sim_s = [], [], [], [], [], []
    ssd_p, ssd_s, cs_p, cs_s, v_s = [], [], [], [], []
    for i in range(DEPTH):
        lw = [w[i] for w in (norm_pre_g, w_in, conv_a_w, conv_a_b, ln_a_g, ln_a_b,
                             s5_a_re, s5_a_im, s5_log_step, s5_b_re, s5_b_im, s5_c_re, s5_c_im, s5_d,
                             s5_w_glu, s5_b_glu,
                             ssd_conv_w, ssd_conv_b, ssd_dt_bias, ssd_a_log, ssd_d, ssd_norm_g,
                             mlp_ln_g, mlp_ln_b, mlp_w_s, mlp_b_s,
                             w_branch, w_out, norm_post_g)]
        z_conv_a = jnp.zeros((nb, CONV_A_WIDTH - 1, W_BRANCH), hp.dtype)
        z_s5 = jnp.zeros((nb, S5_GROUPS, S5_STATE), jnp.float32)
        z_ssd = jnp.zeros((nb, SSD_GROUPS, SSD_HPG, SSD_HEADDIM, SSD_STATE), jnp.float32)
        z_conv_ssd = jnp.zeros((nb, SSD_CONV - 1, SSD_CONV_DIM), hp.dtype)
        hp, a1, r1, m1, s1, c1, _ = trunk_layer(hp, z_conv_a, z_s5, z_s5, z_ssd, z_conv_ssd, *lw)
        hs, a2, r2, m2, s2, c2, v2 = trunk_layer(hs, cache_conv_a[i], state_s5_re[i], state_s5_im[i],
                                                 state_ssd[i], cache_conv_ssd[i], *lw)
        ca_p.append(a1); ca_s.append(a2); sre_p.append(r1); sre_s.append(r2)
        sim_p.append(m1); sim_s.append(m2); ssd_p.append(s1); ssd_s.append(s2)
        cs_p.append(c1); cs_s.append(c2); v_s.append(v2)
    return (hp, hs,
            jnp.stack(ca_p), jnp.stack(ca_s),
            jnp.stack(sre_p), jnp.stack(sre_s),
            jnp.stack(sim_p), jnp.stack(sim_s),
            jnp.stack(ssd_p), jnp.stack(ssd_s),
            jnp.stack(cs_p), jnp.stack(cs_s),
            jnp.stack(v_s))
```

```python
import functools
import math

import jax
import jax.numpy as jnp
from jax import lax
from jax.experimental import pallas as pl
from jax.experimental.pallas import tpu as pltpu

F32 = jnp.float32
BF16 = jnp.bfloat16
EPS = 1e-6

D_MODEL = 4096
W_BRANCH = 2048
N_BRANCH = 4
CONV_A_WIDTH = 31
S5_GROUPS = 128
S5_GROUP = 16
S5_STATE = 64
S5_LANES = S5_GROUPS * S5_STATE
S5_GB = 8
S5_NB = S5_GROUPS // S5_GB
SSD_HEADS = 32
SSD_GROUPS = 8
SSD_STATE = 128
SSD_CHUNK = 128
SSD_CONV_DIM = 4096
MLP_CHUNK = 128
MLP_GROUPS = 16
DT_PAD = 128
N_MAIN = 11 * W_BRANCH
COL_A_VAL, COL_A_G, COL_A_GATE, COL_B_U, COL_B_GATE, COL_C_Z = 0, 1, 2, 3, 4, 5
COL_XBC4096 = 3
COL_D_U, COL_D_V, COL_D_GATE = 8, 9, 10

VMEM_LIMIT = 56 * 1024 * 1024


def _cp(*sem):
    return pltpu.CompilerParams(dimension_semantics=sem, vmem_limit_bytes=VMEM_LIMIT)


def _sigmoid(x):
    return jax.nn.sigmoid(x)


def _silu(x):
    return x * jax.nn.sigmoid(x)


def _gelu(x):
    return jax.nn.gelu(x)


def _dot(a, b):
    return jnp.dot(a, b, preferred_element_type=F32)


def _dot_nt(a, b):
    return lax.dot_general(a, b, (((1,), (1,)), ((), ())), preferred_element_type=F32)


def _dot_exact_rhs(a_bf16, x):
    hi = x.astype(BF16)
    r1 = x - hi.astype(F32)
    mid = r1.astype(BF16)
    lo = (r1 - mid.astype(F32)).astype(BF16)
    return _dot(a_bf16, hi) + _dot(a_bf16, mid) + _dot(a_bf16, lo)


def _norm_kernel(x_ref, g_ref, wdt_ref, xn_ref, dt_ref):
    x = x_ref[...]
    y = x * lax.rsqrt(jnp.mean(x * x, axis=-1, keepdims=True) + EPS) * g_ref[...]
    xb = y.astype(BF16)
    xn_ref[...] = xb
    dt_ref[...] = _dot(xb, wdt_ref[...])


def _norm_call(x, g, wdt, tm):
    m = x.shape[0]
    return pl.pallas_call(
        _norm_kernel,
        grid=(m // tm,),
        in_specs=[pl.BlockSpec((tm, D_MODEL), lambda i: (i, 0)),
                  pl.BlockSpec((1, D_MODEL), lambda i: (0, 0)),
                  pl.BlockSpec((D_MODEL, DT_PAD), lambda i: (0, 0))],
        out_specs=[pl.BlockSpec((tm, D_MODEL), lambda i: (i, 0)),
                   pl.BlockSpec((tm, DT_PAD), lambda i: (i, 0))],
        out_shape=[jax.ShapeDtypeStruct((m, D_MODEL), BF16),
                   jax.ShapeDtypeStruct((m, DT_PAD), F32)],
        compiler_params=_cp("arbitrary"),
        name="norm_dt",
    )(x, g, wdt)


def _mm_kernel(x_ref, w_ref, o_ref):
    o_ref[...] = _dot(x_ref[...], w_ref[...]).astype(o_ref.dtype)


def _inproj_call(xn, wmain, tm, tn):
    m, k = xn.shape
    return pl.pallas_call(
        _mm_kernel,
        grid=(m // tm, N_MAIN // tn),
        in_specs=[pl.BlockSpec((tm, k), lambda i, j: (i, 0)),
                  pl.BlockSpec((k, tn), lambda i, j: (0, j))],
        out_specs=pl.BlockSpec((tm, tn), lambda i, j: (i, j)),
        out_shape=jax.ShapeDtypeStruct((m, N_MAIN), F32),
        compiler_params=_cp("arbitrary", "arbitrary"),
        name="in_proj",
    )(xn, wmain)


CA_TC = 128
CA_HALO = 32
CA_RB = 32
CA_LW = 256


def _ln_swish_gate(v, lg, lb, gate):
    mu = jnp.mean(v, axis=-1, keepdims=True)
    xc = v - mu
    y = xc * lax.rsqrt(jnp.mean(xc * xc, axis=-1, keepdims=True) + EPS) * lg + lb
    return _silu(y) * _silu(gate)


def _conva_prompt_kernel(av_ref, ag_ref, gate_ref, w_ref, b_ref, lg_ref, lb_ref,
                         o_ref, nc_ref, ext_ref, cv_ref):
    c = pl.program_id(1)
    tc = CA_TC

    @pl.when(c == 0)
    def _():
        ext_ref[0:CA_HALO, :] = jnp.zeros((CA_HALO, W_BRANCH), F32)
        ext_ref[CA_HALO + tc:CA_HALO + tc + 8, :] = jnp.zeros((8, W_BRANCH), F32)

    ext_ref[CA_HALO:CA_HALO + tc, :] = av_ref[...] * _sigmoid(ag_ref[...])

    off = CA_HALO - (CONV_A_WIDTH - 1)
    for lb in range(W_BRANCH // CA_LW):
        ls = slice(lb * CA_LW, (lb + 1) * CA_LW)
        for rb in range(tc // CA_RB):
            r0 = rb * CA_RB
            acc = jnp.broadcast_to(b_ref[:, ls], (CA_RB, CA_LW))
            for r in range(8):
                part = None
                for q in range((off + CONV_A_WIDTH + 7) // 8):
                    k = 8 * q + r - off
                    if 0 <= k < CONV_A_WIDTH:
                        term = w_ref[k:k + 1, ls] * ext_ref[r0 + 8 * q:r0 + 8 * q + CA_RB + 8, ls]
                        part = term if part is None else part + term
                acc = acc + part[r:r + CA_RB]
            cv_ref[r0:r0 + CA_RB, ls] = acc

    o_ref[...] = _ln_swish_gate(cv_ref[...], lg_ref[...], lb_ref[...], gate_ref[...]).astype(o_ref.dtype)

    @pl.when(c == pl.num_programs(1) - 1)
    def _():
        nc_ref[...] = ext_ref[tc + off:tc + CA_HALO, :]

    ext_ref[0:CA_HALO, :] = ext_ref[tc:tc + CA_HALO, :]


def _conva_prompt_call(proj, n_seq, seq, conv_w, conv_b, ln_g, ln_b):
    nck = seq // CA_TC
    row = lambda n, c: n * nck + c
    vec = pl.BlockSpec((1, W_BRANCH), lambda n, c: (0, 0))
    return pl.pallas_call(
        _conva_prompt_kernel,
        grid=(n_seq, nck),
        in_specs=[pl.BlockSpec((CA_TC, W_BRANCH), lambda n, c: (row(n, c), COL_A_VAL)),
                  pl.BlockSpec((CA_TC, W_BRANCH), lambda n, c: (row(n, c), COL_A_G)),
                  pl.BlockSpec((CA_TC, W_BRANCH), lambda n, c: (row(n, c), COL_A_GATE)),
                  pl.BlockSpec((CONV_A_WIDTH, W_BRANCH), lambda n, c: (0, 0)),
                  vec, vec, vec],
        out_specs=[pl.BlockSpec((CA_TC, W_BRANCH), lambda n, c: (row(n, c), 0)),
                   pl.BlockSpec((None, CONV_A_WIDTH - 1, W_BRANCH), lambda n, c: (n, 0, 0))],
        out_shape=[jax.ShapeDtypeStruct((n_seq * seq, W_BRANCH), BF16),
                   jax.ShapeDtypeStruct((n_seq, CONV_A_WIDTH - 1, W_BRANCH), F32)],
        scratch_shapes=[pltpu.VMEM((CA_HALO + CA_TC + 8, W_BRANCH), F32),
                        pltpu.VMEM((CA_TC, W_BRANCH), F32)],
        compiler_params=_cp("arbitrary", "arbitrary"),
        name="conv_a_prompt",
    )(proj, proj, proj, conv_w, conv_b, ln_g, ln_b)


SAMPLE_NB = 8


def _conva_sample_kernel(av_ref, ag_ref, gate_ref, cache_ref, w_ref, b_ref, lg_ref, lb_ref,
                         o_ref, nc_ref):
    h = av_ref[...] * _sigmoid(ag_ref[...])
    kw = CONV_A_WIDTH - 1
    acc = b_ref[...] + w_ref[kw:kw + 1, :] * h
    for k in range(kw):
        acc = acc + w_ref[k:k + 1, :] * cache_ref[:, k, :]
    o_ref[...] = _ln_swish_gate(acc, lg_ref[...], lb_ref[...], gate_ref[...]).astype(o_ref.dtype)
    nc_ref[:, 0:kw - 1, :] = cache_ref[:, 1:kw, :]
    nc_ref[:, kw - 1, :] = h


def _conva_sample_call(proj, cache, conv_w, conv_b, ln_g, ln_b):
    ns = proj.shape[0]
    nb = SAMPLE_NB
    vec = pl.BlockSpec((1, W_BRANCH), lambda i: (0, 0))
    cspec = pl.BlockSpec((nb, CONV_A_WIDTH - 1, W_BRANCH), lambda i: (i, 0, 0))
    return pl.pallas_call(
        _conva_sample_kernel,
        grid=(ns // nb,),
        in_specs=[pl.BlockSpec((nb, W_BRANCH), lambda i: (i, COL_A_VAL)),
                  pl.BlockSpec((nb, W_BRANCH), lambda i: (i, COL_A_G)),
                  pl.BlockSpec((nb, W_BRANCH), lambda i: (i, COL_A_GATE)),
                  cspec,
                  pl.BlockSpec((CONV_A_WIDTH, W_BRANCH), lambda i: (0, 0)),
                  vec, vec, vec],
        out_specs=[pl.BlockSpec((nb, W_BRANCH), lambda i: (i, 0)), cspec],
        out_shape=[jax.ShapeDtypeStruct((ns, W_BRANCH), BF16),
                   jax.ShapeDtypeStruct(cache.shape, F32)],
        compiler_params=_cp("arbitrary"),
        name="conv_a_sample",
    )(proj, proj, proj, cache, conv_w, conv_b, ln_g, ln_b)


def _s5_prep_kernel(are_ref, aim_ref, ls_ref, bre_ref, bim_ref,
                    abre_ref, abim_ref, bbre_ref, bbim_ref):
    lam_re = are_ref[...]
    lam_im = aim_ref[...]
    step = jnp.exp(ls_ref[...])
    mag = jnp.exp(lam_re * step)
    ab_re = mag * jnp.cos(lam_im * step)
    ab_im = mag * jnp.sin(lam_im * step)
    den = lam_re * lam_re + lam_im * lam_im
    nr = ab_re - 1.0
    coef_re = (nr * lam_re + ab_im * lam_im) / den
    coef_im = (ab_im * lam_re - nr * lam_im) / den
    br = bre_ref[...]
    bi = bim_ref[...]
    abre_ref[...] = ab_re
    abim_ref[...] = ab_im
    bbre_ref[...] = coef_re * br - coef_im * bi
    bbim_ref[...] = coef_re * bi + coef_im * br


def _s5_prep_call(a_re_x, a_im_x, step_x, b_re2, b_im2):
    g, n = a_re_x.shape
    gb = 16
    spec = pl.BlockSpec((gb, n), lambda i: (i, 0))
    sh = jax.ShapeDtypeStruct((g, n), F32)
    return pl.pallas_call(
        _s5_prep_kernel,
        grid=(g // gb,),
        in_specs=[spec] * 5,
        out_specs=[spec] * 4,
        out_shape=[sh] * 4,
        compiler_params=_cp("arbitrary"),
        name="s5_prep",
    )(a_re_x, a_im_x, step_x, b_re2, b_im2)


S5_TC = 256
S5_LW = 512
S5_BW = S5_GB * S5_STATE
S5_UW = S5_GB * S5_GROUP


def _cmul(ar, ai, br, bi):
    return ar * br - ai * bi, ar * bi + ai * br


def _s5_tables(abr_ref, abi_ref, lvr, lvi, pwr, pwi):
    lw = 1024
    for lb in range(S5_LANES // lw):
        ls = slice(lb * lw, (lb + 1) * lw)
        row = lax.broadcasted_iota(jnp.int32, (8, lw), 0)
        a1r = jnp.broadcast_to(abr_ref[:, ls], (8, lw))
        a1i = jnp.broadcast_to(abi_ref[:, ls], (8, lw))
        cr, ci = a1r, a1i
        pr = jnp.zeros((8, lw), F32)
        pi = jnp.zeros((8, lw), F32)
        level = 0
        for r in range(8):
            pr = jnp.where(row == r, cr, pr)
            pi = jnp.where(row == r, ci, pi)
            if r + 1 in (1, 2, 4):
                lvr[level, :, ls] = jnp.where(row >= r + 1, cr, 0.0)
                lvi[level, :, ls] = jnp.where(row >= r + 1, ci, 0.0)
                level += 1
            cr, ci = _cmul(cr, ci, a1r, a1i)
        pwr[:, ls] = pr
        pwi[:, ls] = pi


def _s5_prompt_kernel(u_ref, abr_ref, abi_ref, wbr_ref, wbi_ref, wcr_ref, wci_ref, d_ref,
                      y_ref, sre_ref, sim_ref,
                      hre, him, lvr, lvi, pwr, pwi, car, cai):
    n = pl.program_id(0)
    c = pl.program_id(1)
    tc = S5_TC

    @pl.when((n == 0) & (c == 0))
    def _():
        _s5_tables(abr_ref, abi_ref, lvr, lvi, pwr, pwi)

    @pl.when(c == 0)
    def _():
        car[...] = jnp.zeros(car.shape, F32)
        cai[...] = jnp.zeros(cai.shape, F32)

    for j in range(S5_NB):
        ub = u_ref[:, j * S5_UW:(j + 1) * S5_UW].astype(BF16)
        hre[:, j * S5_BW:(j + 1) * S5_BW] = _dot(ub, wbr_ref[j])
        him[:, j * S5_BW:(j + 1) * S5_BW] = _dot(ub, wbi_ref[j])

    for lb in range(S5_LANES // S5_LW):
        ls = slice(lb * S5_LW, (lb + 1) * S5_LW)

        def tile(i, carry, ls=ls):
            cr, ci = carry
            r0 = pl.multiple_of(i * 8, 8)
            xr = hre[pl.ds(r0, 8), ls]
            xi = him[pl.ds(r0, 8), ls]
            for level, sh in enumerate((1, 2, 4)):
                sr = pltpu.roll(xr, sh, 0)
                si = pltpu.roll(xi, sh, 0)
                tr, ti = _cmul(lvr[level, :, ls], lvi[level, :, ls], sr, si)
                xr = xr + tr
                xi = xi + ti
            tr, ti = _cmul(pwr[:, ls], pwi[:, ls], cr, ci)
            xr = xr + tr
            xi = xi + ti
            hre[pl.ds(r0, 8), ls] = xr
            him[pl.ds(r0, 8), ls] = xi
            return (jnp.broadcast_to(xr[7:8, :], (8, S5_LW)), jnp.broadcast_to(xi[7:8, :], (8, S5_LW)))

        cr, ci = lax.fori_loop(0, tc // 8, tile, (car[:, ls], cai[:, ls]))
        car[:, ls] = cr
        cai[:, ls] = ci

    for j in range(S5_NB):
        hr = hre[:, j * S5_BW:(j + 1) * S5_BW].astype(BF16)
        hi = him[:, j * S5_BW:(j + 1) * S5_BW].astype(BF16)
        us = slice(j * S5_UW, (j + 1) * S5_UW)
        y_ref[:, us] = _dot(hr, wcr_ref[j]) - _dot(hi, wci_ref[j]) + d_ref[:, us] * u_ref[:, us]

    @pl.when(c == pl.num_programs(1) - 1)
    def _():
        sre_ref[...] = car[0:1, :]
        sim_ref[...] = cai[0:1, :]


def _s5_weight_specs(imap):
    return [pl.BlockSpec((1, S5_LANES), imap),
            pl.BlockSpec((1, S5_LANES), imap),
            pl.BlockSpec((S5_NB, S5_UW, S5_BW), lambda *a: (0, 0, 0)),
            pl.BlockSpec((S5_NB, S5_UW, S5_BW), lambda *a: (0, 0, 0)),
            pl.BlockSpec((S5_NB, S5_BW, S5_UW), lambda *a: (0, 0, 0)),
            pl.BlockSpec((S5_NB, S5_BW, S5_UW), lambda *a: (0, 0, 0)),
            pl.BlockSpec((1, W_BRANCH), imap)]


def _s5_prompt_call(proj, n_seq, seq, ab_re, ab_im, wb_re, wb_im, wc_re, wc_im, dvec):
    nck = seq // S5_TC
    row = lambda n, c: n * nck + c
    sspec = pl.BlockSpec((None, 1, S5_LANES), lambda n, c: (n, 0, 0))
    return pl.pallas_call(
        _s5_prompt_kernel,
        grid=(n_seq, nck),
        in_specs=[pl.BlockSpec((S5_TC, W_BRANCH), lambda n, c: (row(n, c), COL_B_U))]
        + _s5_weight_specs(lambda n, c: (0, 0)),
        out_specs=[pl.BlockSpec((S5_TC, W_BRANCH), lambda n, c: (row(n, c), 0)), sspec, sspec],
        out_shape=[jax.ShapeDtypeStruct((n_seq * seq, W_BRANCH), F32),
                   jax.ShapeDtypeStruct((n_seq, 1, S5_LANES), F32),
                   jax.ShapeDtypeStruct((n_seq, 1, S5_LANES), F32)],
        scratch_shapes=[pltpu.VMEM((S5_TC, S5_LANES), F32), pltpu.VMEM((S5_TC, S5_LANES), F32),
                        pltpu.VMEM((3, 8, S5_LANES), F32), pltpu.VMEM((3, 8, S5_LANES), F32),
                        pltpu.VMEM((8, S5_LANES), F32), pltpu.VMEM((8, S5_LANES), F32),
                        pltpu.VMEM((8, S5_LANES), F32), pltpu.VMEM((8, S5_LANES), F32)],
        compiler_params=_cp("arbitrary", "arbitrary"),
        name="s5_prompt",
    )(proj, ab_re, ab_im, wb_re, wb_im, wc_re, wc_im, dvec)


def _s5_sample_kernel(u_ref, h0r_ref, h0i_ref, abr_ref, abi_ref, wbr_ref, wbi_ref, wcr_ref, wci_ref,
                      d_ref, y_ref, sre_ref, sim_ref):
    for j in range(S5_NB):
        us = slice(j * S5_UW, (j + 1) * S5_UW)
        ss = slice(j * S5_BW, (j + 1) * S5_BW)
        ub = u_ref[:, us].astype(BF16)
        ar = abr_ref[:, ss]
        ai = abi_ref[:, ss]
        h0r = h0r_ref[:, ss]
        h0i = h0i_ref[:, ss]
        nr = ar * h0r - ai * h0i + _dot(ub, wbr_ref[j])
        ni = ar * h0i + ai * h0r + _dot(ub, wbi_ref[j])
        sre_ref[:, ss] = nr
        sim_ref[:, ss] = ni
        y_ref[:, us] = (_dot(nr.astype(BF16), wcr_ref[j]) - _dot(ni.astype(BF16), wci_ref[j])
                        + d_ref[:, us] * u_ref[:, us])


def _s5_sample_call(proj, h0_re, h0_im, ab_re, ab_im, wb_re, wb_im, wc_re, wc_im, dvec):
    ns = proj.shape[0]
    tb = 64
    sspec = pl.BlockSpec((tb, S5_LANES), lambda i: (i, 0))
    return pl.pallas_call(
        _s5_sample_kernel,
        grid=(ns // tb,),
        in_specs=[pl.BlockSpec((tb, W_BRANCH), lambda i: (i, COL_B_U)), sspec, sspec]
        + _s5_weight_specs(lambda i: (0, 0)),
        out_specs=[pl.BlockSpec((tb, W_BRANCH), lambda i: (i, 0)), sspec, sspec],
        out_shape=[jax.ShapeDtypeStruct((ns, W_BRANCH), F32),
                   jax.ShapeDtypeStruct((ns, S5_LANES), F32),
                   jax.ShapeDtypeStruct((ns, S5_LANES), F32)],
        compiler_params=_cp("arbitrary"),
        name="s5_sample",
    )(proj, h0_re, h0_im, ab_re, ab_im, wb_re, wb_im, wc_re, wc_im, dvec)


def _s5_glu_kernel(y_ref, gate_ref, w_ref, b_ref, o_ref):
    g = _gelu(y_ref[...])
    z = _dot(g.astype(BF16), w_ref[...]) + b_ref[...]
    o_ref[...] = (g * _sigmoid(z) * _silu(gate_ref[...])).astype(o_ref.dtype)


def _s5_glu_call(y, proj, w_glu, b_glu, tm):
    m = y.shape[0]
    return pl.pallas_call(
        _s5_glu_kernel,
        grid=(m // tm,),
        in_specs=[pl.BlockSpec((tm, W_BRANCH), lambda i: (i, 0)),
                  pl.BlockSpec((tm, W_BRANCH), lambda i: (i, COL_B_GATE)),
                  pl.BlockSpec((W_BRANCH, W_BRANCH), lambda i: (0, 0)),
                  pl.BlockSpec((1, W_BRANCH), lambda i: (0, 0))],
        out_specs=pl.BlockSpec((tm, W_BRANCH), lambda i: (i, 0)),
        out_shape=jax.ShapeDtypeStruct((m, W_BRANCH), BF16),
        compiler_params=_cp("arbitrary"),
        name="s5_glu",
    )(y, proj, w_glu, b_glu)


SSD_EXT_HEAD = 8
SSD_XS_OFF = 0
SSD_B_OFF = W_BRANCH
SSD_C_OFF = W_BRANCH + SSD_GROUPS * SSD_STATE
SSD_GW = W_BRANCH // SSD_GROUPS


def _softplus(x):
    return jnp.maximum(x, 0.0) + jnp.log(1.0 + jnp.exp(-jnp.abs(x)))


def _group_rmsnorm(y, g_ref, o_ref):
    for g in range(SSD_GROUPS):
        gs = slice(g * SSD_GW, (g + 1) * SSD_GW)
        yg = y[:, gs]
        o_ref[:, gs] = (yg * lax.rsqrt(jnp.mean(yg * yg, axis=-1, keepdims=True) + EPS)
                        * g_ref[:, gs]).astype(o_ref.dtype)


def _ssd_prompt_kernel(z_ref, xbc_ref, dtr_ref, cw_ref, cb_ref, dtb_ref, alog_ref, dvec_ref, ng_ref,
                       o_ref, st_ref, nb_ref,
                       ext, h, xa, ybuf):
    c = pl.program_id(1)
    q = SSD_CHUNK
    e0 = SSD_EXT_HEAD

    @pl.when(c == 0)
    def _():
        ext[0:e0, :] = jnp.zeros((e0, SSD_CONV_DIM), F32)
        h[...] = jnp.zeros(h.shape, F32)

    ext[e0:e0 + q, :] = xbc_ref[...]

    lw, rb = 512, 32
    for lb in range(SSD_CONV_DIM // lw):
        ls = slice(lb * lw, (lb + 1) * lw)
        for r0 in range(0, q, rb):
            win = ext[r0:r0 + rb + e0, ls]
            acc = cb_ref[:, ls] + cw_ref[3:4, ls] * win[e0:e0 + rb]
            for k in range(3):
                acc = acc + cw_ref[k:k + 1, ls] * win[e0 - 3 + k:e0 - 3 + k + rb]
            xa[r0:r0 + rb, ls] = _silu(acc)

    @pl.when(c == pl.num_programs(1) - 1)
    def _():
        nb_ref[...] = ext[e0 + q - 3:e0 + q, :]

    ext[0:e0, :] = ext[q:q + e0, :]

    dt = _softplus(dtr_ref[...] + dtb_ref[...])
    a = -jnp.exp(alog_ref[...])
    da = dt * a
    row = lax.broadcasted_iota(jnp.int32, (q, q), 0)
    col = lax.broadcasted_iota(jnp.int32, (q, q), 1)
    tri = row >= col
    lt = jnp.where(tri, 1.0, 0.0).astype(BF16)
    dacs = _dot_exact_rhs(lt, da)
    dcs_t = dacs.T
    dt_t = dt.T
    tot_t = jnp.broadcast_to(dcs_t[:, q - 1:q], (q, q))
    w_t = dt_t * jnp.exp(tot_t - dcs_t)
    cd_t = jnp.exp(tot_t)
    lane_lo = col < 64
    row_lo = row < 64

    def rows_of(m, r):
        return jnp.broadcast_to(m[r:r + 1, :], (q, q))

    for g in range(SSD_GROUPS):
        bm = xa[:, SSD_B_OFF + g * SSD_STATE:SSD_B_OFF + (g + 1) * SSD_STATE]
        cm = xa[:, SSD_C_OFF + g * SSD_STATE:SSD_C_OFF + (g + 1) * SSD_STATE]
        bm_b = bm.astype(BF16)
        cm_b = cm.astype(BF16)
        cb = _dot_nt(cm_b, bm_b)
        for p in range(2):
            r_lo = 4 * g + 2 * p
            cs = slice(g * SSD_GW + p * 128, g * SSD_GW + (p + 1) * 128)
            xs = xa[:, cs]
            ydiag = None
            eo = []
            for hh in range(2):
                r = r_lo + hh
                a_col = jnp.broadcast_to(dacs[:, r:r + 1], (q, q))
                seg = a_col - rows_of(dcs_t, r)
                lmat = jnp.where(tri, jnp.exp(seg), 0.0)
                mp = cb * lmat * rows_of(dt_t, r)
                keep = lane_lo if hh == 0 else jnp.logical_not(lane_lo)
                xs_h = jnp.where(keep, xs, 0.0)
                term = _dot(mp.astype(BF16), xs_h.astype(BF16))
                ydiag = term if ydiag is None else ydiag + term
                eo.append(jnp.exp(a_col))
            eo_pair = jnp.where(lane_lo, eo[0], eo[1])
            hs = slice(p * 128, (p + 1) * 128)
            h_pair = h[g, hs, :]
            yoff = _dot_nt(cm_b, h_pair.astype(BF16)) * eo_pair
            ybuf[:, cs] = ydiag + yoff + dvec_ref[:, cs] * xs
            w_rows = jnp.where(row_lo, rows_of(w_t, r_lo), rows_of(w_t, r_lo + 1))
            cd_rows = jnp.where(row_lo, rows_of(cd_t, r_lo), rows_of(cd_t, r_lo + 1))
            st = _dot((xs.T * w_rows).astype(BF16), bm_b)
            h[g, hs, :] = h_pair * cd_rows + st

    y = ybuf[...] * _silu(z_ref[...])
    _group_rmsnorm(y, ng_ref, o_ref)

    @pl.when(c == pl.num_programs(1) - 1)
    def _():
        st_ref[...] = h[...]


def _ssd_prompt_call(proj, dtraw, n_seq, seq, conv_w, conv_b, dt_bias, a_log, dvec, norm_g):
    q = SSD_CHUNK
    nck = seq // q
    row = lambda n, c: n * nck + c
    c0 = lambda n, c: (0, 0)
    return pl.pallas_call(
        _ssd_prompt_kernel,
        grid=(n_seq, nck),
        in_specs=[pl.BlockSpec((q, W_BRANCH), lambda n, c: (row(n, c), COL_C_Z)),
                  pl.BlockSpec((q, SSD_CONV_DIM), lambda n, c: (row(n, c), COL_XBC4096)),
                  pl.BlockSpec((q, DT_PAD), lambda n, c: (row(n, c), 0)),
                  pl.BlockSpec((4, SSD_CONV_DIM), c0),
                  pl.BlockSpec((1, SSD_CONV_DIM), c0),
                  pl.BlockSpec((1, DT_PAD), c0),
                  pl.BlockSpec((1, DT_PAD), c0),
                  pl.BlockSpec((1, W_BRANCH), c0),
                  pl.BlockSpec((1, W_BRANCH), c0)],
        out_specs=[pl.BlockSpec((q, W_BRANCH), lambda n, c: (row(n, c), 0)),
                   pl.BlockSpec((None, SSD_GROUPS, SSD_GW, SSD_STATE), lambda n, c: (n, 0, 0, 0)),
                   pl.BlockSpec((None, 3, SSD_CONV_DIM), lambda n, c: (n, 0, 0))],
        out_shape=[jax.ShapeDtypeStruct((n_seq * seq, W_BRANCH), BF16),
                   jax.ShapeDtypeStruct((n_seq, SSD_GROUPS, SSD_GW, SSD_STATE), F32),
                   jax.ShapeDtypeStruct((n_seq, 3, SSD_CONV_DIM), F32)],
        scratch_shapes=[pltpu.VMEM((SSD_EXT_HEAD + q, SSD_CONV_DIM), F32),
                        pltpu.VMEM((SSD_GROUPS, SSD_GW, SSD_STATE), F32),
                        pltpu.VMEM((q, SSD_CONV_DIM), F32),
                        pltpu.VMEM((q, W_BRANCH), F32)],
        compiler_params=_cp("arbitrary", "arbitrary"),
        name="ssd_prompt",
    )(proj, proj, dtraw, conv_w, conv_b, dt_bias, a_log, dvec, norm_g)


def _ssd_sample_kernel(z_ref, xbc_ref, dtr_ref, buf_ref, h0_ref, cw_ref, cb_ref, dtb_ref, alog_ref,
                       dvec_ref, ng_ref, ex_ref,
                       o_ref, st_ref, nb_ref, colx, cold):
    nb = SAMPLE_NB
    xbc = xbc_ref[...]
    acc = cb_ref[...] + cw_ref[3:4, :] * xbc
    for k in range(3):
        acc = acc + cw_ref[k:k + 1, :] * buf_ref[:, k, :]
    xa = _silu(acc)
    nb_ref[:, 0:2, :] = buf_ref[:, 1:3, :]
    nb_ref[:, 2, :] = xbc

    dt = _softplus(dtr_ref[...] + dtb_ref[...])
    a = -jnp.exp(alog_ref[...])
    dec = jnp.exp(dt * a)
    ex = ex_ref[...]
    dt_x = _dot_exact_rhs_t(dt, ex)
    dec_x = _dot_exact_rhs_t(dec, ex)
    xs = xa[:, 0:W_BRANCH]
    xdt = xs * dt_x

    pad = jnp.zeros((128 - nb, W_BRANCH), F32)
    colx[...] = jnp.concatenate([xdt, pad], axis=0).T
    cold[...] = jnp.concatenate([dec_x, pad], axis=0).T

    rown = lax.broadcasted_iota(jnp.int32, (nb, SSD_GW), 0)
    for g in range(SSD_GROUPS):
        gs = slice(g * SSD_GW, (g + 1) * SSD_GW)
        bm = xa[:, SSD_B_OFF + g * SSD_STATE:SSD_B_OFF + (g + 1) * SSD_STATE]
        cm = xa[:, SSD_C_OFF + g * SSD_STATE:SSD_C_OFF + (g + 1) * SSD_STATE]
        cm_b = cm.astype(BF16)
        cbv = jnp.sum(cm * bm, axis=-1, keepdims=True)
        yoff = jnp.zeros((nb, SSD_GW), F32)
        for n in range(nb):
            h0 = h0_ref[n, g]
            xcol = jnp.broadcast_to(colx[gs, n:n + 1], (SSD_GW, SSD_STATE))
            dcol = jnp.broadcast_to(cold[gs, n:n + 1], (SSD_GW, SSD_STATE))
            st_ref[n, g] = h0 * dcol + xcol * jnp.broadcast_to(bm[n:n + 1, :], (SSD_GW, SSD_STATE))
            yoff = yoff + jnp.where(rown == n, _dot_nt(cm_b, h0.astype(BF16)), 0.0)
        ybuf_g = yoff * dec_x[:, gs] + cbv * xdt[:, gs] + dvec_ref[:, gs] * xs[:, gs]
        yg = ybuf_g * _silu(z_ref[:, gs])
        o_ref[:, gs] = (yg * lax.rsqrt(jnp.mean(yg * yg, axis=-1, keepdims=True) + EPS)
                        * ng_ref[:, gs]).astype(o_ref.dtype)


def _dot_exact_rhs_t(x, e_bf16):
    hi = x.astype(BF16)
    r1 = x - hi.astype(F32)
    mid = r1.astype(BF16)
    lo = (r1 - mid.astype(F32)).astype(BF16)
    return _dot(hi, e_bf16) + _dot(mid, e_bf16) + _dot(lo, e_bf16)


def _ssd_sample_call(proj, dtraw, buf, h0, conv_w, conv_b, dt_bias, a_log, dvec, norm_g, expand):
    ns = proj.shape[0]
    nb = SAMPLE_NB
    c0 = lambda i: (0, 0)
    hspec = pl.BlockSpec((nb, SSD_GROUPS, SSD_GW, SSD_STATE), lambda i: (i, 0, 0, 0))
    bspec = pl.BlockSpec((nb, 3, SSD_CONV_DIM), lambda i: (i, 0, 0))
    return pl.pallas_call(
        _ssd_sample_kernel,
        grid=(ns // nb,),
        in_specs=[pl.BlockSpec((nb, W_BRANCH), lambda i: (i, COL_C_Z)),
                  pl.BlockSpec((nb, SSD_CONV_DIM), lambda i: (i, COL_XBC4096)),
                  pl.BlockSpec((nb, DT_PAD), lambda i: (i, 0)),
                  bspec, hspec,
                  pl.BlockSpec((4, SSD_CONV_DIM), c0),
                  pl.BlockSpec((1, SSD_CONV_DIM), c0),
                  pl.BlockSpec((1, DT_PAD), c0),
                  pl.BlockSpec((1, DT_PAD), c0),
                  pl.BlockSpec((1, W_BRANCH), c0),
                  pl.BlockSpec((1, W_BRANCH), c0),
                  pl.BlockSpec((DT_PAD, W_BRANCH), c0)],
        out_specs=[pl.BlockSpec((nb, W_BRANCH), lambda i: (i, 0)), hspec, bspec],
        out_shape=[jax.ShapeDtypeStruct((ns, W_BRANCH), BF16),
                   jax.ShapeDtypeStruct(h0.shape, F32),
                   jax.ShapeDtypeStruct(buf.shape, F32)],
        scratch_shapes=[pltpu.VMEM((W_BRANCH, 128), F32), pltpu.VMEM((W_BRANCH, 128), F32)],
        compiler_params=_cp("arbitrary"),
        name="ssd_sample",
    )(proj, proj, dtraw, buf, h0, conv_w, conv_b, dt_bias, a_log, dvec, norm_g, expand)


MLP_GD = W_BRANCH // MLP_GROUPS


def _layer_norm(x, g, b):
    mu = jnp.mean(x, axis=-1, keepdims=True)
    xc = x - mu
    return xc * lax.rsqrt(jnp.mean(xc * xc, axis=-1, keepdims=True) + EPS) * g + b


def _mlp_prompt_kernel(u_ref, v_ref, gate_ref, lg_ref, lb_ref, ws_ref, bst_ref, o_ref, vd):
    q = MLP_CHUNK
    vd[...] = _layer_norm(_gelu(v_ref[...]), lg_ref[...], lb_ref[...])
    row = lax.broadcasted_iota(jnp.int32, (q, q), 0)
    col = lax.broadcasted_iota(jnp.int32, (q, q), 1)
    tri = row >= col
    for g in range(MLP_GROUPS):
        gs = slice(g * MLP_GD, (g + 1) * MLP_GD)
        ws = jnp.where(tri, ws_ref[g], 0.0).astype(BF16)
        mixed = _dot(ws, vd[:, gs].astype(BF16)) + jnp.broadcast_to(bst_ref[:, g:g + 1], (q, MLP_GD))
        o_ref[:, gs] = (_gelu(u_ref[:, gs]) * mixed * _silu(gate_ref[:, gs])).astype(o_ref.dtype)


def _mlp_prompt_call(proj, n_seq, seq, ln_g, ln_b, w_s, b_s_t):
    q = MLP_CHUNK
    nck = seq // q
    row = lambda n, c: n * nck + c
    c0 = lambda n, c: (0, 0)
    return pl.pallas_call(
        _mlp_prompt_kernel,
        grid=(n_seq, nck),
        in_specs=[pl.BlockSpec((q, W_BRANCH), lambda n, c: (row(n, c), COL_D_U)),
                  pl.BlockSpec((q, W_BRANCH), lambda n, c: (row(n, c), COL_D_V)),
                  pl.BlockSpec((q, W_BRANCH), lambda n, c: (row(n, c), COL_D_GATE)),
                  pl.BlockSpec((1, W_BRANCH), c0),
                  pl.BlockSpec((1, W_BRANCH), c0),
                  pl.BlockSpec((MLP_GROUPS, q, q), lambda n, c: (0, 0, 0)),
                  pl.BlockSpec((q, 128), c0)],
        out_specs=pl.BlockSpec((q, W_BRANCH), lambda n, c: (row(n, c), 0)),
        out_shape=jax.ShapeDtypeStruct((n_seq * seq, W_BRANCH), BF16),
        scratch_shapes=[pltpu.VMEM((q, W_BRANCH), F32)],
        compiler_params=_cp("arbitrary", "arbitrary"),
        name="mlp_prompt",
    )(proj, proj, proj, ln_g, ln_b, w_s, b_s_t)


def _mlp_sample_kernel(u_ref, v_ref, gate_ref, lg_ref, lb_ref, w0_ref, b0_ref, o_ref, vd_ref):
    vd = _layer_norm(_gelu(v_ref[...]), lg_ref[...], lb_ref[...])
    vd_ref[...] = vd
    mixed = w0_ref[...] * vd + b0_ref[...]
    o_ref[...] = (_gelu(u_ref[...]) * mixed * _silu(gate_ref[...])).astype(o_ref.dtype)


def _mlp_sample_call(proj, ln_g, ln_b, w0_x, b0_x):
    ns = proj.shape[0]
    tb = 64
    c0 = lambda i: (0, 0)
    vec = pl.BlockSpec((1, W_BRANCH), c0)
    return pl.pallas_call(
        _mlp_sample_kernel,
        grid=(ns // tb,),
        in_specs=[pl.BlockSpec((tb, W_BRANCH), lambda i: (i, COL_D_U)),
                  pl.BlockSpec((tb, W_BRANCH), lambda i: (i, COL_D_V)),
                  pl.BlockSpec((tb, W_BRANCH), lambda i: (i, COL_D_GATE)),
                  vec, vec, vec, vec],
        out_specs=[pl.BlockSpec((tb, W_BRANCH), lambda i: (i, 0)),
                   pl.BlockSpec((tb, W_BRANCH), lambda i: (i, 0))],
        out_shape=[jax.ShapeDtypeStruct((ns, W_BRANCH), BF16),
                   jax.ShapeDtypeStruct((ns, W_BRANCH), F32)],
        compiler_params=_cp("arbitrary"),
        name="mlp_sample",
    )(proj, proj, proj, ln_g, ln_b, w0_x, b0_x)


def _merge_kernel(xn_ref, oa_ref, ob_ref, oc_ref, od_ref,
                  wm0_ref, wm1_ref, wm2_ref, wm3_ref, wb_ref, o_ref):
    xn = xn_ref[...]
    acc = None
    for b, (o_b, wm) in enumerate(((oa_ref, wm0_ref), (ob_ref, wm1_ref), (oc_ref, wm2_ref), (od_ref, wm3_ref))):
        gate = _sigmoid(_dot(xn, wm[...]))
        term = gate * _dot(o_b[...], wb_ref[b])
        acc = term if acc is None else acc + term
    o_ref[...] = acc.astype(o_ref.dtype)


def _merge_call(xn, outs, wmain, w_branch, tm, tn):
    m = xn.shape[0]
    nj = D_MODEL // tn
    base = N_MAIN // tn

    def wm_spec(b):
        return pl.BlockSpec((D_MODEL, tn), lambda i, j: (0, base + b * nj + j))

    ospec = pl.BlockSpec((tm, W_BRANCH), lambda i, j: (i, 0))
    return pl.pallas_call(
        _merge_kernel,
        grid=(m // tm, nj),
        in_specs=[pl.BlockSpec((tm, D_MODEL), lambda i, j: (i, 0)), ospec, ospec, ospec, ospec,
                  wm_spec(0), wm_spec(1), wm_spec(2), wm_spec(3),
                  pl.BlockSpec((N_BRANCH, W_BRANCH, tn), lambda i, j: (0, 0, j))],
        out_specs=pl.BlockSpec((tm, tn), lambda i, j: (i, j)),
        out_shape=jax.ShapeDtypeStruct((m, D_MODEL), BF16),
        compiler_params=_cp("arbitrary", "arbitrary"),
        name="merge",
    )(xn, *outs, wmain, wmain, wmain, wmain, w_branch)


def _out_kernel(m_ref, w_ref, x_ref, g_ref, o_ref):
    j = pl.program_id(1)
    tn = w_ref.shape[1]
    o_ref[:, pl.ds(pl.multiple_of(j * tn, tn), tn)] = _dot(m_ref[...], w_ref[...])

    @pl.when(j == pl.num_programs(1) - 1)
    def _():
        z = o_ref[...]
        inv = lax.rsqrt(jnp.mean(z * z, axis=-1, keepdims=True) + EPS)
        o_ref[...] = x_ref[...] + z * inv * g_ref[...]


def _out_call(merged, w_out, x, g, tm, tn):
    m = x.shape[0]
    nj = D_MODEL // tn
    return pl.pallas_call(
        _out_kernel,
        grid=(m // tm, nj),
        in_specs=[pl.BlockSpec((tm, D_MODEL), lambda i, j: (i, 0)),
                  pl.BlockSpec((D_MODEL, tn), lambda i, j: (0, j)),
                  pl.BlockSpec((tm, D_MODEL), lambda i, j: (i, 0)),
                  pl.BlockSpec((1, D_MODEL), lambda i, j: (0, 0))],
        out_specs=pl.BlockSpec((tm, D_MODEL), lambda i, j: (i, 0)),
        out_shape=jax.ShapeDtypeStruct((m, D_MODEL), F32),
        compiler_params=_cp("arbitrary", "arbitrary"),
        name="out_proj",
    )(merged, w_out, x, g)


def _block_diag(mats):
    nbk, gb, r, c = mats.shape
    eye = jnp.eye(gb, dtype=mats.dtype)
    out = mats[:, :, :, None, :] * eye[None, :, None, :, None]
    return out.reshape(nbk, gb * r, gb * c)


def _layer_weights(i, w):
    lw = {}
    w_in = w["w_in"][i]
    lw["wmain"] = jnp.concatenate([w_in[:, :16384], w_in[:, 16384 + SSD_HEADS:]], axis=1).astype(BF16)
    lw["wdt"] = jnp.pad(w_in[:, 16384:16384 + SSD_HEADS], ((0, 0), (0, DT_PAD - SSD_HEADS))).astype(BF16)
    lw["norm_pre_g"] = w["norm_pre_g"][i].reshape(1, D_MODEL)
    lw["conv_a_w"] = w["conv_a_w"][i]
    for k in ("conv_a_b", "ln_a_g", "ln_a_b", "s5_b_glu", "ssd_norm_g", "mlp_ln_g", "mlp_ln_b"):
        lw[k] = w[k][i].reshape(1, W_BRANCH)

    rep = lambda t: jnp.repeat(t, S5_GROUP, axis=1)
    step_x = jnp.broadcast_to(w["s5_log_step"][i][:, None], (S5_GROUPS, S5_STATE * S5_GROUP))
    ab_re_x, ab_im_x, bb_re2, bb_im2 = _s5_prep_call(
        rep(w["s5_a_re"][i]), rep(w["s5_a_im"][i]), step_x,
        w["s5_b_re"][i].reshape(S5_GROUPS, -1), w["s5_b_im"][i].reshape(S5_GROUPS, -1))
    lw["ab_re"] = ab_re_x[:, ::S5_GROUP].reshape(1, S5_LANES)
    lw["ab_im"] = ab_im_x[:, ::S5_GROUP].reshape(1, S5_LANES)
    tb = lambda t: jnp.swapaxes(t.reshape(S5_NB, S5_GB, S5_STATE, S5_GROUP), 2, 3)
    lw["wb_re"] = _block_diag(tb(bb_re2)).astype(BF16)
    lw["wb_im"] = _block_diag(tb(bb_im2)).astype(BF16)
    tcm = lambda t: jnp.swapaxes(t.reshape(S5_NB, S5_GB, S5_GROUP, S5_STATE), 2, 3)
    lw["wc_re"] = _block_diag(tcm(w["s5_c_re"][i])).astype(BF16)
    lw["wc_im"] = _block_diag(tcm(w["s5_c_im"][i])).astype(BF16)
    lw["s5_d"] = w["s5_d"][i].reshape(1, W_BRANCH)
    lw["s5_w_glu"] = w["s5_w_glu"][i].astype(BF16)

    lw["ssd_conv_w"] = w["ssd_conv_w"][i]
    lw["ssd_conv_b"] = w["ssd_conv_b"][i].reshape(1, SSD_CONV_DIM)
    padh = lambda t: jnp.pad(t.reshape(1, SSD_HEADS), ((0, 0), (0, DT_PAD - SSD_HEADS)))
    lw["ssd_dt_bias"] = padh(w["ssd_dt_bias"][i])
    lw["ssd_a_log"] = padh(w["ssd_a_log"][i])
    lw["ssd_d"] = jnp.repeat(w["ssd_d"][i], W_BRANCH // SSD_HEADS).reshape(1, W_BRANCH)

    lw["mlp_w_s"] = w["mlp_w_s"][i]
    lw["mlp_b_s_t"] = jnp.pad(w["mlp_b_s"][i].T, ((0, 0), (0, 128 - MLP_GROUPS)))
    lw["mlp_w0"] = jnp.repeat(w["mlp_w_s"][i][:, 0, 0], MLP_GD).reshape(1, W_BRANCH)
    lw["mlp_b0"] = jnp.repeat(w["mlp_b_s"][i][:, 0], MLP_GD).reshape(1, W_BRANCH)

    lw["w_branch"] = w["w_branch"][i].astype(BF16)
    lw["w_out"] = w["w_out"][i].astype(BF16)
    lw["norm_post_g"] = w["norm_post_g"][i].reshape(1, D_MODEL)
    return lw


def _head_expand():
    r = jnp.arange(DT_PAD)[:, None]
    c = jnp.arange(W_BRANCH)[None, :] // (W_BRANCH // SSD_HEADS)
    return (r == c).astype(BF16)


def _dense_tail(x, xn, outs, lw, tm_merge, tn_merge, tm_out, tn_out):
    merged = _merge_call(xn, outs, lw["wmain"], lw["w_branch"], tm_merge, tn_merge)
    return _out_call(merged, lw["w_out"], x, lw["norm_post_g"], tm_out, tn_out)


def _prompt_layer(x, n_seq, seq, lw):
    xn, dtraw = _norm_call(x, lw["norm_pre_g"], lw["wdt"], 256)
    proj = _inproj_call(xn, lw["wmain"], 1024, 1024)
    out_a, conv_a = _conva_prompt_call(proj, n_seq, seq, lw["conv_a_w"], lw["conv_a_b"], lw["ln_a_g"], lw["ln_a_b"])
    y_b, s_re, s_im = _s5_prompt_call(proj, n_seq, seq, lw["ab_re"], lw["ab_im"], lw["wb_re"], lw["wb_im"],
                                      lw["wc_re"], lw["wc_im"], lw["s5_d"])
    out_b = _s5_glu_call(y_b, proj, lw["s5_w_glu"], lw["s5_b_glu"], 512)
    out_c, ssd_st, conv_ssd = _ssd_prompt_call(proj, dtraw, n_seq, seq, lw["ssd_conv_w"], lw["ssd_conv_b"],
                                               lw["ssd_dt_bias"], lw["ssd_a_log"], lw["ssd_d"], lw["ssd_norm_g"])
    out_d = _mlp_prompt_call(proj, n_seq, seq, lw["mlp_ln_g"], lw["mlp_ln_b"], lw["mlp_w_s"], lw["mlp_b_s_t"])
    y = _dense_tail(x, xn, (out_a, out_b, out_c, out_d), lw, 512, 256, 512, 512)
    return y, conv_a, s_re, s_im, ssd_st, conv_ssd


def _sample_layer(x, cache_a, h_re, h_im, h_ssd, buf_ssd, lw, expand):
    ns = x.shape[0]
    xn, dtraw = _norm_call(x, lw["norm_pre_g"], lw["wdt"], ns)
    proj = _inproj_call(xn, lw["wmain"], ns, 1024)
    out_a, conv_a = _conva_sample_call(proj, cache_a, lw["conv_a_w"], lw["conv_a_b"], lw["ln_a_g"], lw["ln_a_b"])
    y_b, s_re, s_im = _s5_sample_call(proj, h_re, h_im, lw["ab_re"], lw["ab_im"], lw["wb_re"], lw["wb_im"],
                                      lw["wc_re"], lw["wc_im"], lw["s5_d"])
    out_b = _s5_glu_call(y_b, proj, lw["s5_w_glu"], lw["s5_b_glu"], ns)
    out_c, ssd_st, conv_ssd = _ssd_sample_call(proj, dtraw, buf_ssd, h_ssd, lw["ssd_conv_w"], lw["ssd_conv_b"],
                                               lw["ssd_dt_bias"], lw["ssd_a_log"], lw["ssd_d"], lw["ssd_norm_g"],
                                               expand)
    out_d, v_d = _mlp_sample_call(proj, lw["mlp_ln_g"], lw["mlp_ln_b"], lw["mlp_w0"], lw["mlp_b0"])
    y = _dense_tail(x, xn, (out_a, out_b, out_c, out_d), lw, ns, 256, ns, 512)
    return y, conv_a, s_re, s_im, ssd_st, conv_ssd, v_d


def kernel(x_prompt, x_sample, cache_conv_a, state_s5_re, state_s5_im, state_ssd, cache_conv_ssd, norm_pre_g, w_in, conv_a_w, conv_a_b, ln_a_g, ln_a_b, s5_a_re, s5_a_im, s5_log_step, s5_b_re, s5_b_im, s5_c_re, s5_c_im, s5_d, s5_w_glu, s5_b_glu, ssd_conv_w, ssd_conv_b, ssd_dt_bias, ssd_a_log, ssd_d, ssd_norm_g, mlp_ln_g, mlp_ln_b, mlp_w_s, mlp_b_s, w_branch, w_out, norm_post_g):
    w = dict(norm_pre_g=norm_pre_g, w_in=w_in, conv_a_w=conv_a_w, conv_a_b=conv_a_b, ln_a_g=ln_a_g,
             ln_a_b=ln_a_b, s5_a_re=s5_a_re, s5_a_im=s5_a_im, s5_log_step=s5_log_step, s5_b_re=s5_b_re,
             s5_b_im=s5_b_im, s5_c_re=s5_c_re, s5_c_im=s5_c_im, s5_d=s5_d, s5_w_glu=s5_w_glu,
             s5_b_glu=s5_b_glu, ssd_conv_w=ssd_conv_w, ssd_conv_b=ssd_conv_b, ssd_dt_bias=ssd_dt_bias,
             ssd_a_log=ssd_a_log, ssd_d=ssd_d, ssd_norm_g=ssd_norm_g, mlp_ln_g=mlp_ln_g, mlp_ln_b=mlp_ln_b,
             mlp_w_s=mlp_w_s, mlp_b_s=mlp_b_s, w_branch=w_branch, w_out=w_out, norm_post_g=norm_post_g)
    depth = w_in.shape[0]
    n_seq, seq, _ = x_prompt.shape
    ns = x_sample.shape[0]
    hp = x_prompt.reshape(n_seq * seq, D_MODEL)
    hs = x_sample.reshape(ns, D_MODEL)
    expand = _head_expand()
    acc = [[] for _ in range(11)]
    for i in range(depth):
        lw = _layer_weights(i, w)
        hp, a1, r1, m1, s1, c1 = _prompt_layer(hp, n_seq, seq, lw)
        hs, a2, r2, m2, s2, c2, v2 = _sample_layer(
            hs, cache_conv_a[i],
            state_s5_re[i].reshape(ns, S5_LANES), state_s5_im[i].reshape(ns, S5_LANES),
            state_ssd[i].reshape(ns, SSD_GROUPS, SSD_GW, SSD_STATE), cache_conv_ssd[i], lw, expand)
        vals = (a1, a2,
                r1.reshape(n_seq, S5_GROUPS, S5_STATE), r2.reshape(ns, S5_GROUPS, S5_STATE),
                m1.reshape(n_seq, S5_GROUPS, S5_STATE), m2.reshape(ns, S5_GROUPS, S5_STATE),
                s1.reshape(n_seq, SSD_GROUPS, 4, 64, SSD_STATE), s2.reshape(ns, SSD_GROUPS, 4, 64, SSD_STATE),
                c1, c2, v2.reshape(ns, 1, W_BRANCH))
        for lst, v in zip(acc, vals):
            lst.append(v)
    return (hp.reshape(n_seq, seq, D_MODEL), hs.reshape(ns, 1, D_MODEL)) + tuple(jnp.stack(l) for l in acc)
```

```python
import jax
import jax.numpy as jnp
from jax import lax
from jax.experimental import pallas as pl
from jax.experimental.pallas import tpu as pltpu

F32 = jnp.float32
BF16 = jnp.bfloat16
EPS = 1e-6

D_MODEL = 4096
W_BRANCH = 2048
N_BRANCH = 4
LANE = 128
CONV_A_WIDTH = 31
S5_GROUPS = 128
S5_GROUP = 16
S5_STATE = 64
S5_LANES = S5_GROUPS * S5_STATE
S5_GB = 8
S5_NB = S5_GROUPS // S5_GB
S5_BW = S5_GB * S5_STATE
S5_UW = S5_GB * S5_GROUP
S5_FOLD = 4
SSD_HEADS = 32
SSD_GROUPS = 8
SSD_STATE = 128
SSD_CHUNK = 128
SSD_CONV_DIM = 4096
SSD_GW = W_BRANCH // SSD_GROUPS
SSD_B_OFF = W_BRANCH
SSD_C_OFF = W_BRANCH + SSD_GROUPS * SSD_STATE
MLP_CHUNK = 128
MLP_GROUPS = 16
MLP_GD = W_BRANCH // MLP_GROUPS
DT_PAD = LANE
WA_BRANCH_COLS = 16384
WD_START = WA_BRANCH_COLS + SSD_HEADS
WD_BRANCH_COLS = 3 * W_BRANCH
WM_START = WD_START + WD_BRANCH_COLS
COL_A_VAL, COL_A_G, COL_A_GATE, COL_B_U, COL_B_GATE, COL_C_Z = 0, 1, 2, 3, 4, 5
COL_XBC4096 = 3
COL_D_U, COL_D_V, COL_D_GATE = 0, 1, 2

VMEM_LIMIT = 56 * 1024 * 1024


def _cp(*sem):
    return pltpu.CompilerParams(dimension_semantics=sem, vmem_limit_bytes=VMEM_LIMIT)


def _vec_spec(width, layer):
    return pl.BlockSpec((None, 1, width), lambda *_: (layer, 0, 0))


def _layer_spec(shape, layer):
    return pl.BlockSpec((None,) + tuple(shape), lambda *_: (layer,) + (0,) * len(shape))


def _sigmoid(x):
    return jax.nn.sigmoid(x)


def _silu(x):
    return x * jax.nn.sigmoid(x)


def _gelu(x):
    return jax.nn.gelu(x)


def _dot(a, b):
    return jnp.dot(a, b, preferred_element_type=F32)


def _dot_nt(a, b):
    return lax.dot_general(a, b, (((1,), (1,)), ((), ())), preferred_element_type=F32)


def _split3(x):
    hi = x.astype(BF16)
    r1 = x - hi.astype(F32)
    mid = r1.astype(BF16)
    lo = (r1 - mid.astype(F32)).astype(BF16)
    return hi, mid, lo


def _dot_exact_rhs(a_bf16, x):
    hi, mid, lo = _split3(x)
    return _dot(a_bf16, hi) + _dot(a_bf16, mid) + _dot(a_bf16, lo)


def _dot_exact_lhs(x, e_bf16):
    hi, mid, lo = _split3(x)
    return _dot(hi, e_bf16) + _dot(mid, e_bf16) + _dot(lo, e_bf16)


def _norm_kernel(x_ref, g_ref, wdt_ref, xn_ref, dt_ref):
    x = x_ref[...]
    y = x * lax.rsqrt(jnp.mean(x * x, axis=-1, keepdims=True) + EPS) * g_ref[...]
    xb = y.astype(BF16)
    xn_ref[...] = xb
    d = _dot_nt(xb, wdt_ref[...])
    lane = lax.broadcasted_iota(jnp.int32, d.shape, 1)
    dt_ref[...] = jnp.where(lane < SSD_HEADS, d, 0.0)


def _norm_call(x, g, w_t, layer, tm):
    m = x.shape[0]
    return pl.pallas_call(
        _norm_kernel,
        grid=(m // tm,),
        in_specs=[pl.BlockSpec((tm, D_MODEL), lambda i: (i, 0)),
                  _vec_spec(D_MODEL, layer),
                  pl.BlockSpec((None, DT_PAD, D_MODEL), lambda i: (layer, WA_BRANCH_COLS // DT_PAD, 0))],
        out_specs=[pl.BlockSpec((tm, D_MODEL), lambda i: (i, 0)),
                   pl.BlockSpec((tm, DT_PAD), lambda i: (i, 0))],
        out_shape=[jax.ShapeDtypeStruct((m, D_MODEL), BF16),
                   jax.ShapeDtypeStruct((m, DT_PAD), F32)],
        compiler_params=_cp("arbitrary"),
        name="norm_dt",
    )(x, g, w_t)


W_ROW_ALIGN = 32


def _wt_spec(layer, row0, tn):
    return pl.BlockSpec((pl.Element(1), pl.Element(tn), pl.Element(D_MODEL)),
                        lambda i, j: (layer, pl.multiple_of(row0 + j * tn, W_ROW_ALIGN), 0))


def _mm_kernel(x_ref, w_ref, o_ref):
    o_ref[...] = _dot_nt(x_ref[...], w_ref[0]).astype(o_ref.dtype)


def _inproj_call(xn, w_t, layer, row0, n_cols, tm, tn):
    m, k = xn.shape
    return pl.pallas_call(
        _mm_kernel,
        grid=(m // tm, n_cols // tn),
        in_specs=[pl.BlockSpec((tm, k), lambda i, j: (i, 0)),
                  _wt_spec(layer, row0, tn)],
        out_specs=pl.BlockSpec((tm, tn), lambda i, j: (i, j)),
        out_shape=jax.ShapeDtypeStruct((m, n_cols), F32),
        compiler_params=_cp("arbitrary", "arbitrary"),
        name="in_proj",
    )(xn, w_t)


CA_TC = 128
CA_HALO = 32
CA_RB = 64
W_TILES = W_BRANCH // LANE


def _ln_swish_gate(v, lg, lb, gate):
    mu = jnp.mean(v, axis=-1, keepdims=True)
    xc = v - mu
    y = xc * lax.rsqrt(jnp.mean(xc * xc, axis=-1, keepdims=True) + EPS) * lg + lb
    return _silu(y) * _silu(gate)


def _conva_prompt_kernel(av_ref, ag_ref, gate_ref, w_ref, b_ref, lg_ref, lb_ref,
                         o_ref, nc_ref, ext_ref, cv_ref):
    c = pl.program_id(1)
    tc = CA_TC
    off = CA_HALO - (CONV_A_WIDTH - 1)

    @pl.when(c == 0)
    def _():
        ext_ref[:, 0:CA_HALO, :] = jnp.zeros((W_TILES, CA_HALO, LANE), F32)

    for lt in range(W_TILES):
        ls = slice(lt * LANE, (lt + 1) * LANE)
        ext_ref[lt, CA_HALO:CA_HALO + tc, :] = av_ref[:, ls] * _sigmoid(ag_ref[:, ls])

    def conv_tile(lt, carry):
        ls = pl.ds(pl.multiple_of(lt * LANE, LANE), LANE)
        for r0 in range(0, tc, CA_RB):
            accs = [jnp.broadcast_to(b_ref[:, ls], (8, LANE))] * (CA_RB // 8)
            for k in range(CONV_A_WIDTH):
                wk = jnp.broadcast_to(w_ref[k:k + 1, ls], (8, LANE))
                accs = [a + wk * ext_ref[lt, r0 + off + k + 8 * i:r0 + off + k + 8 * i + 8, :]
                        for i, a in enumerate(accs)]
            for i, a in enumerate(accs):
                cv_ref[r0 + 8 * i:r0 + 8 * i + 8, ls] = a
        return carry

    lax.fori_loop(0, W_TILES, conv_tile, 0)

    o_ref[...] = _ln_swish_gate(cv_ref[...], lg_ref[...], lb_ref[...], gate_ref[...]).astype(o_ref.dtype)

    @pl.when(c == pl.num_programs(1) - 1)
    def _():
        for lt in range(W_TILES):
            nc_ref[:, lt * LANE:(lt + 1) * LANE] = ext_ref[lt, tc + off:tc + CA_HALO, :]

    for lt in range(W_TILES):
        ext_ref[lt, 0:CA_HALO, :] = ext_ref[lt, tc:tc + CA_HALO, :]


def _conva_prompt_call(proj, n_seq, seq, layer, p):
    nck = seq // CA_TC
    row = lambda n, c: n * nck + c
    vec = _vec_spec(W_BRANCH, layer)
    return pl.pallas_call(
        _conva_prompt_kernel,
        grid=(n_seq, nck),
        in_specs=[pl.BlockSpec((CA_TC, W_BRANCH), lambda n, c: (row(n, c), COL_A_VAL)),
                  pl.BlockSpec((CA_TC, W_BRANCH), lambda n, c: (row(n, c), COL_A_G)),
                  pl.BlockSpec((CA_TC, W_BRANCH), lambda n, c: (row(n, c), COL_A_GATE)),
                  _layer_spec((CONV_A_WIDTH, W_BRANCH), layer),
                  vec, vec, vec],
        out_specs=[pl.BlockSpec((CA_TC, W_BRANCH), lambda n, c: (row(n, c), 0)),
                   pl.BlockSpec((None, CONV_A_WIDTH - 1, W_BRANCH), lambda n, c: (n, 0, 0))],
        out_shape=[jax.ShapeDtypeStruct((n_seq * seq, W_BRANCH), BF16),
                   jax.ShapeDtypeStruct((n_seq, CONV_A_WIDTH - 1, W_BRANCH), F32)],
        scratch_shapes=[pltpu.VMEM((W_TILES, CA_HALO + CA_TC, LANE), F32),
                        pltpu.VMEM((CA_TC, W_BRANCH), F32)],
        compiler_params=_cp("arbitrary", "arbitrary"),
        name="conv_a_prompt",
    )(proj, proj, proj, p["conv_a_w"], p["conv_a_b"], p["ln_a_g"], p["ln_a_b"])


SAMPLE_NB = 8


def _conva_sample_kernel(av_ref, ag_ref, gate_ref, cache_ref, w_ref, b_ref, lg_ref, lb_ref,
                         o_ref, nc_ref):
    h = av_ref[...] * _sigmoid(ag_ref[...])
    kw = CONV_A_WIDTH - 1
    acc = b_ref[...] + w_ref[kw:kw + 1, :] * h
    for k in range(kw):
        acc = acc + w_ref[k:k + 1, :] * cache_ref[k]
    o_ref[...] = _ln_swish_gate(acc, lg_ref[...], lb_ref[...], gate_ref[...]).astype(o_ref.dtype)
    for k in range(kw - 1):
        nc_ref[k] = cache_ref[k + 1]
    nc_ref[kw - 1] = h


def _conva_sample_call(proj, cache_t, layer, p):
    ns = proj.shape[0]
    nb = SAMPLE_NB
    kw = CONV_A_WIDTH - 1
    vec = _vec_spec(W_BRANCH, layer)
    return pl.pallas_call(
        _conva_sample_kernel,
        grid=(ns // nb,),
        in_specs=[pl.BlockSpec((nb, W_BRANCH), lambda i: (i, COL_A_VAL)),
                  pl.BlockSpec((nb, W_BRANCH), lambda i: (i, COL_A_G)),
                  pl.BlockSpec((nb, W_BRANCH), lambda i: (i, COL_A_GATE)),
                  pl.BlockSpec((None, kw, nb, W_BRANCH), lambda i: (layer, 0, i, 0)),
                  _layer_spec((CONV_A_WIDTH, W_BRANCH), layer),
                  vec, vec, vec],
        out_specs=[pl.BlockSpec((nb, W_BRANCH), lambda i: (i, 0)),
                   pl.BlockSpec((kw, nb, W_BRANCH), lambda i: (0, i, 0))],
        out_shape=[jax.ShapeDtypeStruct((ns, W_BRANCH), BF16),
                   jax.ShapeDtypeStruct((kw, ns, W_BRANCH), F32)],
        compiler_params=_cp("arbitrary"),
        name="conv_a_sample",
    )(proj, proj, proj, cache_t, p["conv_a_w"], p["conv_a_b"], p["ln_a_g"], p["ln_a_b"])


def _cmul(ar, ai, br, bi):
    return ar * br - ai * bi, ar * bi + ai * br


def _s5_prep_kernel(are_ref, aim_ref, ls_ref, bre_ref, bim_ref, abre_ref, abim_ref, *bb_refs):
    lam_re = are_ref[...]
    lam_im = aim_ref[...]
    step = jnp.exp(ls_ref[...])
    mag = jnp.exp(lam_re * step)
    ab_re = mag * jnp.cos(lam_im * step)
    ab_im = mag * jnp.sin(lam_im * step)
    den = lam_re * lam_re + lam_im * lam_im
    nr = ab_re - 1.0
    coef_re = (nr * lam_re + ab_im * lam_im) / den
    coef_im = (ab_im * lam_re - nr * lam_im) / den
    abre_ref[...] = ab_re
    abim_ref[...] = ab_im
    cur_re, cur_im = _cmul(coef_re, coef_im, bre_ref[...], bim_ref[...])
    for k in range(S5_FOLD):
        bb_refs[2 * k][...] = cur_re
        bb_refs[2 * k + 1][...] = cur_im
        cur_re, cur_im = _cmul(ab_re, ab_im, cur_re, cur_im)


def _s5_prep_call(a_re_x, a_im_x, step_x, b_re2, b_im2):
    g, n = a_re_x.shape
    gb = 16
    spec = pl.BlockSpec((gb, n), lambda i: (i, 0))
    sh = jax.ShapeDtypeStruct((g, n), F32)
    n_out = 2 + 2 * S5_FOLD
    return pl.pallas_call(
        _s5_prep_kernel,
        grid=(g // gb,),
        in_specs=[spec] * 5,
        out_specs=[spec] * n_out,
        out_shape=[sh] * n_out,
        compiler_params=_cp("arbitrary"),
        name="s5_prep",
    )(a_re_x, a_im_x, step_x, b_re2, b_im2)


S5_TC = 1024
S5_UHEAD = 8


def _s5_prompt_kernel(u_ref, abr_ref, abi_ref, wbr_ref, wbi_ref, wcr_ref, wci_ref, d_ref,
                      y_ref, sre_ref, sim_ref,
                      ush, hre, him, l4r, l4i, pwr, pwi, car, cai):
    c = pl.program_id(2)
    tc = S5_TC
    bw = S5_BW

    @pl.when(c == 0)
    def _():
        row = lax.broadcasted_iota(jnp.int32, (8, bw), 0)
        a1r = jnp.broadcast_to(abr_ref[...], (8, bw))
        a1i = jnp.broadcast_to(abi_ref[...], (8, bw))
        cr, ci = a1r, a1i
        pr = jnp.zeros((8, bw), F32)
        pi = jnp.zeros((8, bw), F32)
        for r in range(8):
            pr = jnp.where(row == r, cr, pr)
            pi = jnp.where(row == r, ci, pi)
            if r + 1 == S5_FOLD:
                l4r[...] = jnp.where(row >= S5_FOLD, cr, 0.0)
                l4i[...] = jnp.where(row >= S5_FOLD, ci, 0.0)
            cr, ci = _cmul(cr, ci, a1r, a1i)
        pwr[...] = pr
        pwi[...] = pi
        car[...] = jnp.zeros(car.shape, F32)
        cai[...] = jnp.zeros(cai.shape, F32)
        ush[0:S5_UHEAD, :] = jnp.zeros((S5_UHEAD, S5_UW), F32)

    u = u_ref[...]
    ush[S5_UHEAD:S5_UHEAD + tc, :] = u
    row8 = lax.broadcasted_iota(jnp.int32, (tc, S5_UW), 0) & 7
    parts = [u.astype(BF16)]
    for k in range(1, S5_FOLD):
        uk = ush[S5_UHEAD - k:S5_UHEAD - k + tc, :]
        parts.append(jnp.where(row8 >= k, uk, 0.0).astype(BF16))
    lhs = jnp.concatenate(parts, axis=-1)
    hre[...] = _dot(lhs, wbr_ref[...])
    him[...] = _dot(lhs, wbi_ref[...])

    def tile(i, carry):
        cr, ci = carry
        r0 = pl.multiple_of(i * 8, 8)
        xr = hre[pl.ds(r0, 8), :]
        xi = him[pl.ds(r0, 8), :]
        tr, ti = _cmul(l4r[...], l4i[...], pltpu.roll(xr, S5_FOLD, 0), pltpu.roll(xi, S5_FOLD, 0))
        xr = xr + tr
        xi = xi + ti
        tr, ti = _cmul(pwr[...], pwi[...], cr, ci)
        xr = xr + tr
        xi = xi + ti
        hre[pl.ds(r0, 8), :] = xr
        him[pl.ds(r0, 8), :] = xi
        return (jnp.broadcast_to(xr[7:8, :], (8, bw)), jnp.broadcast_to(xi[7:8, :], (8, bw)))

    cr, ci = lax.fori_loop(0, tc // 8, tile, (car[...], cai[...]))
    car[...] = cr
    cai[...] = ci

    y_ref[...] = (_dot(hre[...].astype(BF16), wcr_ref[...]) - _dot(him[...].astype(BF16), wci_ref[...])
                  + d_ref[...] * u)

    @pl.when(c == pl.num_programs(2) - 1)
    def _():
        sre_ref[...] = cr[0:1, :]
        sim_ref[...] = ci[0:1, :]


def _s5_prompt_call(proj, n_seq, seq, layer, p):
    nck = seq // S5_TC
    ub = W_BRANCH // S5_UW
    lane_blk = lambda w: pl.BlockSpec((None, 1, w), lambda n, j, c: (layer, 0, j))
    wspec = lambda r, cc: pl.BlockSpec((None, None, r, cc), lambda n, j, c: (layer, j, 0, 0))
    sspec = pl.BlockSpec((None, 1, S5_BW), lambda n, j, c: (n, 0, j))
    return pl.pallas_call(
        _s5_prompt_kernel,
        grid=(n_seq, S5_NB, nck),
        in_specs=[pl.BlockSpec((S5_TC, S5_UW), lambda n, j, c: (n * nck + c, COL_B_U * ub + j)),
                  lane_blk(S5_BW), lane_blk(S5_BW),
                  wspec(S5_FOLD * S5_UW, S5_BW), wspec(S5_FOLD * S5_UW, S5_BW),
                  wspec(S5_BW, S5_UW), wspec(S5_BW, S5_UW),
                  lane_blk(S5_UW)],
        out_specs=[pl.BlockSpec((S5_TC, S5_UW), lambda n, j, c: (n * nck + c, j)), sspec, sspec],
        out_shape=[jax.ShapeDtypeStruct((n_seq * seq, W_BRANCH), F32),
                   jax.ShapeDtypeStruct((n_seq, 1, S5_LANES), F32),
                   jax.ShapeDtypeStruct((n_seq, 1, S5_LANES), F32)],
        scratch_shapes=[pltpu.VMEM((S5_UHEAD + S5_TC, S5_UW), F32),
                        pltpu.VMEM((S5_TC, S5_BW), F32), pltpu.VMEM((S5_TC, S5_BW), F32)]
        + [pltpu.VMEM((8, S5_BW), F32)] * 6,
        compiler_params=_cp("arbitrary", "arbitrary", "arbitrary"),
        name="s5_prompt",
    )(proj, p["ab_re"], p["ab_im"], p["wb_re"], p["wb_im"], p["wc_re"], p["wc_im"], p["s5_d"])


def _s5_sample_kernel(u_ref, h0r_ref, h0i_ref, abr_ref, abi_ref, wbr_ref, wbi_ref, wcr_ref, wci_ref,
                      d_ref, y_ref, sre_ref, sim_ref):
    for j in range(S5_NB):
        us = slice(j * S5_UW, (j + 1) * S5_UW)
        ss = slice(j * S5_BW, (j + 1) * S5_BW)
        ub = u_ref[:, us].astype(BF16)
        ar = abr_ref[:, ss]
        ai = abi_ref[:, ss]
        h0r = h0r_ref[:, ss]
        h0i = h0i_ref[:, ss]
        nr = ar * h0r - ai * h0i + _dot(ub, wbr_ref[j])
        ni = ar * h0i + ai * h0r + _dot(ub, wbi_ref[j])
        sre_ref[:, ss] = nr
        sim_ref[:, ss] = ni
        y_ref[:, us] = (_dot(nr.astype(BF16), wcr_ref[j]) - _dot(ni.astype(BF16), wci_ref[j])
                        + d_ref[:, us] * u_ref[:, us])


def _s5_sample_call(proj, h0_re, h0_im, layer, p):
    ns = proj.shape[0]
    tb = 64
    h0spec = pl.BlockSpec((None, tb, S5_LANES), lambda i: (layer, i, 0))
    sspec = pl.BlockSpec((tb, S5_LANES), lambda i: (i, 0))
    wb = pl.BlockSpec((None, S5_NB, S5_UW, S5_BW), lambda i: (layer, 0, 0, 0))
    wc = pl.BlockSpec((None, S5_NB, S5_BW, S5_UW), lambda i: (layer, 0, 0, 0))
    return pl.pallas_call(
        _s5_sample_kernel,
        grid=(ns // tb,),
        in_specs=[pl.BlockSpec((tb, W_BRANCH), lambda i: (i, COL_B_U)), h0spec, h0spec,
                  _vec_spec(S5_LANES, layer), _vec_spec(S5_LANES, layer), wb, wb, wc, wc,
                  _vec_spec(W_BRANCH, layer)],
        out_specs=[pl.BlockSpec((tb, W_BRANCH), lambda i: (i, 0)), sspec, sspec],
        out_shape=[jax.ShapeDtypeStruct((ns, W_BRANCH), F32),
                   jax.ShapeDtypeStruct((ns, S5_LANES), F32),
                   jax.ShapeDtypeStruct((ns, S5_LANES), F32)],
        compiler_params=_cp("arbitrary"),
        name="s5_sample",
    )(proj, h0_re, h0_im, p["ab_re"], p["ab_im"], p["wb_re"], p["wb_im"], p["wc_re"], p["wc_im"], p["s5_d"])


def _s5_glu_kernel(y_ref, gate_ref, w_ref, b_ref, o_ref):
    g = _gelu(y_ref[...])
    z = _dot(g.astype(BF16), w_ref[...]) + b_ref[...]
    o_ref[...] = (g * _sigmoid(z) * _silu(gate_ref[...])).astype(o_ref.dtype)


def _s5_glu_call(y, proj, layer, p, tm):
    m = y.shape[0]
    return pl.pallas_call(
        _s5_glu_kernel,
        grid=(m // tm,),
        in_specs=[pl.BlockSpec((tm, W_BRANCH), lambda i: (i, 0)),
                  pl.BlockSpec((tm, W_BRANCH), lambda i: (i, COL_B_GATE)),
                  _layer_spec((W_BRANCH, W_BRANCH), layer),
                  _vec_spec(W_BRANCH, layer)],
        out_specs=pl.BlockSpec((tm, W_BRANCH), lambda i: (i, 0)),
        out_shape=jax.ShapeDtypeStruct((m, W_BRANCH), BF16),
        compiler_params=_cp("arbitrary"),
        name="s5_glu",
    )(y, proj, p["s5_w_glu"], p["s5_b_glu"])


SSD_EXT_HEAD = 8
SSD_TAPS = 4
XBC_TILES = SSD_CONV_DIM // LANE


def _softplus(x):
    return jnp.maximum(x, 0.0) + jnp.log(1.0 + jnp.exp(-jnp.abs(x)))


def _ssd_prompt_kernel(z_ref, xbc_ref, dtr_ref, cw_ref, cb_ref, dtb_ref, alog_ref, dvec_ref, ng_ref,
                       o_ref, st_ref, nb_ref,
                       ext, h, xa, ybuf):
    c = pl.program_id(1)
    q = SSD_CHUNK
    e0 = SSD_EXT_HEAD
    off = e0 - (SSD_TAPS - 1)

    @pl.when(c == 0)
    def _():
        ext[:, 0:e0, :] = jnp.zeros((XBC_TILES, e0, LANE), F32)
        h[...] = jnp.zeros(h.shape, F32)

    rb = 64

    def conv_tile(lt, carry):
        ls = pl.ds(pl.multiple_of(lt * LANE, LANE), LANE)
        ext[lt, e0:e0 + q, :] = xbc_ref[:, ls]
        for r0 in range(0, q, rb):
            accs = [jnp.broadcast_to(cb_ref[:, ls], (8, LANE))] * (rb // 8)
            for k in range(SSD_TAPS):
                wk = jnp.broadcast_to(cw_ref[k:k + 1, ls], (8, LANE))
                accs = [a + wk * ext[lt, r0 + off + k + 8 * i:r0 + off + k + 8 * i + 8, :]
                        for i, a in enumerate(accs)]
            for i, a in enumerate(accs):
                xa[r0 + 8 * i:r0 + 8 * i + 8, ls] = _silu(a)
        return carry

    lax.fori_loop(0, XBC_TILES, conv_tile, 0)

    @pl.when(c == pl.num_programs(1) - 1)
    def _():
        for lt in range(XBC_TILES):
            nb_ref[:, lt * LANE:(lt + 1) * LANE] = ext[lt, q + off:q + e0, :]

    for lt in range(XBC_TILES):
        ext[lt, 0:e0, :] = ext[lt, q:q + e0, :]

    dt = _softplus(dtr_ref[...] + dtb_ref[...])
    a = -jnp.exp(alog_ref[...])
    da = dt * a
    row = lax.broadcasted_iota(jnp.int32, (q, q), 0)
    col = lax.broadcasted_iota(jnp.int32, (q, q), 1)
    tri = row >= col
    lt_ones = jnp.where(tri, 1.0, 0.0).astype(BF16)
    dacs = _dot_exact_rhs(lt_ones, da)
    dcs_t = dacs.T
    dt_t = dt.T
    tot_t = jnp.broadcast_to(dcs_t[:, q - 1:q], (q, q))
    w_t = dt_t * jnp.exp(tot_t - dcs_t)
    cd_t = jnp.exp(tot_t)
    lane_lo = col < 64
    row_lo = row < 64

    def rows_of(m, r):
        return jnp.broadcast_to(m[r:r + 1, :], (q, q))

    for g in range(SSD_GROUPS):
        bm_b = xa[:, SSD_B_OFF + g * SSD_STATE:SSD_B_OFF + (g + 1) * SSD_STATE].astype(BF16)
        cm_b = xa[:, SSD_C_OFF + g * SSD_STATE:SSD_C_OFF + (g + 1) * SSD_STATE].astype(BF16)
        cb = _dot_nt(cm_b, bm_b)
        for p in range(2):
            r_lo = 4 * g + 2 * p
            cs = slice(g * SSD_GW + p * LANE, g * SSD_GW + (p + 1) * LANE)
            xs = xa[:, cs]
            ydiag = None
            eo = []
            for hh in range(2):
                r = r_lo + hh
                a_col = jnp.broadcast_to(dacs[:, r:r + 1], (q, q))
                seg = a_col - rows_of(dcs_t, r)
                lmat = jnp.where(tri, jnp.exp(seg), 0.0)
                mp = cb * lmat * rows_of(dt_t, r)
                keep = lane_lo if hh == 0 else jnp.logical_not(lane_lo)
                xs_h = jnp.where(keep, xs, 0.0)
                term = _dot(mp.astype(BF16), xs_h.astype(BF16))
                ydiag = term if ydiag is None else ydiag + term
                eo.append(jnp.exp(a_col))
            eo_pair = jnp.where(lane_lo, eo[0], eo[1])
            hs = slice(p * LANE, (p + 1) * LANE)
            h_pair = h[g, hs, :]
            yoff = _dot_nt(cm_b, h_pair.astype(BF16)) * eo_pair
            ybuf[:, cs] = ydiag + yoff + dvec_ref[:, cs] * xs
            w_rows = jnp.where(row_lo, rows_of(w_t, r_lo), rows_of(w_t, r_lo + 1))
            cd_rows = jnp.where(row_lo, rows_of(cd_t, r_lo), rows_of(cd_t, r_lo + 1))
            st = _dot((xs.T * w_rows).astype(BF16), bm_b)
            h[g, hs, :] = h_pair * cd_rows + st

    y = ybuf[...] * _silu(z_ref[...])
    for g in range(SSD_GROUPS):
        gs = slice(g * SSD_GW, (g + 1) * SSD_GW)
        yg = y[:, gs]
        o_ref[:, gs] = (yg * lax.rsqrt(jnp.mean(yg * yg, axis=-1, keepdims=True) + EPS)
                        * ng_ref[:, gs]).astype(o_ref.dtype)

    @pl.when(c == pl.num_programs(1) - 1)
    def _():
        st_ref[...] = h[...]


def _ssd_prompt_call(proj, dtraw, n_seq, seq, layer, p):
    q = SSD_CHUNK
    nck = seq // q
    row = lambda n, c: n * nck + c
    return pl.pallas_call(
        _ssd_prompt_kernel,
        grid=(n_seq, nck),
        in_specs=[pl.BlockSpec((q, W_BRANCH), lambda n, c: (row(n, c), COL_C_Z)),
                  pl.BlockSpec((q, SSD_CONV_DIM), lambda n, c: (row(n, c), COL_XBC4096)),
                  pl.BlockSpec((q, DT_PAD), lambda n, c: (row(n, c), 0)),
                  _layer_spec((SSD_TAPS, SSD_CONV_DIM), layer),
                  _vec_spec(SSD_CONV_DIM, layer),
                  _vec_spec(DT_PAD, layer), _vec_spec(DT_PAD, layer),
                  _vec_spec(W_BRANCH, layer), _vec_spec(W_BRANCH, layer)],
        out_specs=[pl.BlockSpec((q, W_BRANCH), lambda n, c: (row(n, c), 0)),
                   pl.BlockSpec((None, SSD_GROUPS, SSD_GW, SSD_STATE), lambda n, c: (n, 0, 0, 0)),
                   pl.BlockSpec((None, SSD_TAPS - 1, SSD_CONV_DIM), lambda n, c: (n, 0, 0))],
        out_shape=[jax.ShapeDtypeStruct((n_seq * seq, W_BRANCH), BF16),
                   jax.ShapeDtypeStruct((n_seq, SSD_GROUPS, SSD_GW, SSD_STATE), F32),
                   jax.ShapeDtypeStruct((n_seq, SSD_TAPS - 1, SSD_CONV_DIM), F32)],
        scratch_shapes=[pltpu.VMEM((XBC_TILES, SSD_EXT_HEAD + q, LANE), F32),
                        pltpu.VMEM((SSD_GROUPS, SSD_GW, SSD_STATE), F32),
                        pltpu.VMEM((q, SSD_CONV_DIM), F32),
                        pltpu.VMEM((q, W_BRANCH), F32)],
        compiler_params=_cp("arbitrary", "arbitrary"),
        name="ssd_prompt",
    )(proj, proj, dtraw, p["ssd_conv_w"], p["ssd_conv_b"], p["ssd_dt_bias"], p["ssd_a_log"],
      p["ssd_d"], p["ssd_norm_g"])


def _ssd_sample_kernel(z_ref, xbc_ref, dtr_ref, buf_ref, h0_ref, cw_ref, cb_ref, dtb_ref, alog_ref,
                       dvec_ref, ng_ref, ex_ref,
                       o_ref, st_ref, nb_ref, colx, cold):
    nb = SAMPLE_NB
    xbc = xbc_ref[...]
    acc = cb_ref[...] + cw_ref[SSD_TAPS - 1:SSD_TAPS, :] * xbc
    for k in range(SSD_TAPS - 1):
        acc = acc + cw_ref[k:k + 1, :] * buf_ref[k]
    xa = _silu(acc)
    for k in range(SSD_TAPS - 2):
        nb_ref[k] = buf_ref[k + 1]
    nb_ref[SSD_TAPS - 2] = xbc

    dt = _softplus(dtr_ref[...] + dtb_ref[...])
    a = -jnp.exp(alog_ref[...])
    dec = jnp.exp(dt * a)
    ex = ex_ref[...]
    dt_x = _dot_exact_lhs(dt, ex)
    dec_x = _dot_exact_lhs(dec, ex)
    xs = xa[:, 0:W_BRANCH]
    xdt = xs * dt_x

    pad = jnp.zeros((LANE - nb, W_BRANCH), F32)
    colx[...] = jnp.concatenate([xdt, pad], axis=0).T
    cold[...] = jnp.concatenate([dec_x, pad], axis=0).T

    rown = lax.broadcasted_iota(jnp.int32, (nb, SSD_GW), 0)
    for g in range(SSD_GROUPS):
        gs = slice(g * SSD_GW, (g + 1) * SSD_GW)
        bm = xa[:, SSD_B_OFF + g * SSD_STATE:SSD_B_OFF + (g + 1) * SSD_STATE]
        cm = xa[:, SSD_C_OFF + g * SSD_STATE:SSD_C_OFF + (g + 1) * SSD_STATE]
        cm_b = cm.astype(BF16)
        cbv = jnp.sum(cm * bm, axis=-1, keepdims=True)
        yoff = jnp.zeros((nb, SSD_GW), F32)
        for n in range(nb):
            h0 = h0_ref[n, g]
            xcol = jnp.broadcast_to(colx[gs, n:n + 1], (SSD_GW, SSD_STATE))
            dcol = jnp.broadcast_to(cold[gs, n:n + 1], (SSD_GW, SSD_STATE))
            st_ref[n, g] = h0 * dcol + xcol * jnp.broadcast_to(bm[n:n + 1, :], (SSD_GW, SSD_STATE))
            yoff = yoff + jnp.where(rown == n, _dot_nt(cm_b, h0.astype(BF16)), 0.0)
        yg = (yoff * dec_x[:, gs] + cbv * xdt[:, gs] + dvec_ref[:, gs] * xs[:, gs]) * _silu(z_ref[:, gs])
        o_ref[:, gs] = (yg * lax.rsqrt(jnp.mean(yg * yg, axis=-1, keepdims=True) + EPS)
                        * ng_ref[:, gs]).astype(o_ref.dtype)


def _ssd_sample_call(proj, dtraw, buf_t, h0, layer, p):
    ns = proj.shape[0]
    nb = SAMPLE_NB
    nt = SSD_TAPS - 1
    hshape = (nb, SSD_GROUPS, SSD_GW, SSD_STATE)
    return pl.pallas_call(
        _ssd_sample_kernel,
        grid=(ns // nb,),
        in_specs=[pl.BlockSpec((nb, W_BRANCH), lambda i: (i, COL_C_Z)),
                  pl.BlockSpec((nb, SSD_CONV_DIM), lambda i: (i, COL_XBC4096)),
                  pl.BlockSpec((nb, DT_PAD), lambda i: (i, 0)),
                  pl.BlockSpec((None, nt, nb, SSD_CONV_DIM), lambda i: (layer, 0, i, 0)),
                  pl.BlockSpec((None,) + hshape, lambda i: (layer, i, 0, 0, 0)),
                  _layer_spec((SSD_TAPS, SSD_CONV_DIM), layer),
                  _vec_spec(SSD_CONV_DIM, layer),
                  _vec_spec(DT_PAD, layer), _vec_spec(DT_PAD, layer),
                  _vec_spec(W_BRANCH, layer), _vec_spec(W_BRANCH, layer),
                  pl.BlockSpec((DT_PAD, W_BRANCH), lambda i: (0, 0))],
        out_specs=[pl.BlockSpec((nb, W_BRANCH), lambda i: (i, 0)),
                   pl.BlockSpec(hshape, lambda i: (i, 0, 0, 0)),
                   pl.BlockSpec((nt, nb, SSD_CONV_DIM), lambda i: (0, i, 0))],
        out_shape=[jax.ShapeDtypeStruct((ns, W_BRANCH), BF16),
                   jax.ShapeDtypeStruct((ns, SSD_GROUPS, SSD_GW, SSD_STATE), F32),
                   jax.ShapeDtypeStruct((nt, ns, SSD_CONV_DIM), F32)],
        scratch_shapes=[pltpu.VMEM((W_BRANCH, LANE), F32), pltpu.VMEM((W_BRANCH, LANE), F32)],
        compiler_params=_cp("arbitrary"),
        name="ssd_sample",
    )(proj, proj, dtraw, buf_t, h0, p["ssd_conv_w"], p["ssd_conv_b"], p["ssd_dt_bias"], p["ssd_a_log"],
      p["ssd_d"], p["ssd_norm_g"], p["head_expand"])


def _layer_norm(x, g, b):
    mu = jnp.mean(x, axis=-1, keepdims=True)
    xc = x - mu
    return xc * lax.rsqrt(jnp.mean(xc * xc, axis=-1, keepdims=True) + EPS) * g + b


def _mlp_prompt_kernel(u_ref, v_ref, gate_ref, lg_ref, lb_ref, ws_ref, bst_ref, o_ref, vd):
    q = MLP_CHUNK
    vd[...] = _layer_norm(_gelu(v_ref[...]), lg_ref[...], lb_ref[...])
    row = lax.broadcasted_iota(jnp.int32, (q, q), 0)
    col = lax.broadcasted_iota(jnp.int32, (q, q), 1)
    tri = row >= col
    for g in range(MLP_GROUPS):
        gs = slice(g * MLP_GD, (g + 1) * MLP_GD)
        ws = jnp.where(tri, ws_ref[g], 0.0).astype(BF16)
        mixed = _dot(ws, vd[:, gs].astype(BF16)) + jnp.broadcast_to(bst_ref[:, g:g + 1], (q, MLP_GD))
        o_ref[:, gs] = (_gelu(u_ref[:, gs]) * mixed * _silu(gate_ref[:, gs])).astype(o_ref.dtype)


def _mlp_prompt_call(proj, n_seq, seq, layer, p):
    q = MLP_CHUNK
    nck = seq // q
    row = lambda n, c: n * nck + c
    return pl.pallas_call(
        _mlp_prompt_kernel,
        grid=(n_seq, nck),
        in_specs=[pl.BlockSpec((q, W_BRANCH), lambda n, c: (row(n, c), COL_D_U)),
                  pl.BlockSpec((q, W_BRANCH), lambda n, c: (row(n, c), COL_D_V)),
                  pl.BlockSpec((q, W_BRANCH), lambda n, c: (row(n, c), COL_D_GATE)),
                  _vec_spec(W_BRANCH, layer), _vec_spec(W_BRANCH, layer),
                  _layer_spec((MLP_GROUPS, q, q), layer),
                  _layer_spec((q, LANE), layer)],
        out_specs=pl.BlockSpec((q, W_BRANCH), lambda n, c: (row(n, c), 0)),
        out_shape=jax.ShapeDtypeStruct((n_seq * seq, W_BRANCH), BF16),
        scratch_shapes=[pltpu.VMEM((q, W_BRANCH), F32)],
        compiler_params=_cp("arbitrary", "arbitrary"),
        name="mlp_prompt",
    )(proj, proj, proj, p["mlp_ln_g"], p["mlp_ln_b"], p["mlp_w_s"], p["mlp_b_s_t"])


def _mlp_sample_kernel(u_ref, v_ref, gate_ref, lg_ref, lb_ref, w0_ref, b0_ref, o_ref, vd_ref):
    vd = _layer_norm(_gelu(v_ref[...]), lg_ref[...], lb_ref[...])
    vd_ref[...] = vd
    mixed = w0_ref[...] * vd + b0_ref[...]
    o_ref[...] = (_gelu(u_ref[...]) * mixed * _silu(gate_ref[...])).astype(o_ref.dtype)


def _mlp_sample_call(proj, layer, p):
    ns = proj.shape[0]
    tb = 64
    vec = _vec_spec(W_BRANCH, layer)
    return pl.pallas_call(
        _mlp_sample_kernel,
        grid=(ns // tb,),
        in_specs=[pl.BlockSpec((tb, W_BRANCH), lambda i: (i, COL_D_U)),
                  pl.BlockSpec((tb, W_BRANCH), lambda i: (i, COL_D_V)),
                  pl.BlockSpec((tb, W_BRANCH), lambda i: (i, COL_D_GATE)),
                  vec, vec, vec, vec],
        out_specs=[pl.BlockSpec((tb, W_BRANCH), lambda i: (i, 0)),
                   pl.BlockSpec((tb, W_BRANCH), lambda i: (i, 0))],
        out_shape=[jax.ShapeDtypeStruct((ns, W_BRANCH), BF16),
                   jax.ShapeDtypeStruct((ns, W_BRANCH), F32)],
        compiler_params=_cp("arbitrary"),
        name="mlp_sample",
    )(proj, proj, proj, p["mlp_ln_g"], p["mlp_ln_b"], p["mlp_w0"], p["mlp_b0"])


def _merge_kernel(xn_ref, oa_ref, ob_ref, oc_ref, od_ref,
                  wm0_ref, wm1_ref, wm2_ref, wm3_ref, wb_ref, o_ref):
    xn = xn_ref[...]
    acc = None
    for b, (o_b, wm) in enumerate(((oa_ref, wm0_ref), (ob_ref, wm1_ref), (oc_ref, wm2_ref), (od_ref, wm3_ref))):
        gate = _sigmoid(_dot_nt(xn, wm[0]))
        term = gate * _dot(o_b[...], wb_ref[b])
        acc = term if acc is None else acc + term
    o_ref[...] = acc.astype(o_ref.dtype)


def _merge_call(xn, outs, layer, p, tm, tn):
    m = xn.shape[0]
    nj = D_MODEL // tn

    def wm_spec(b):
        return _wt_spec(layer, WM_START + b * D_MODEL, tn)

    ospec = pl.BlockSpec((tm, W_BRANCH), lambda i, j: (i, 0))
    return pl.pallas_call(
        _merge_kernel,
        grid=(m // tm, nj),
        in_specs=[pl.BlockSpec((tm, D_MODEL), lambda i, j: (i, 0)), ospec, ospec, ospec, ospec,
                  wm_spec(0), wm_spec(1), wm_spec(2), wm_spec(3),
                  pl.BlockSpec((None, N_BRANCH, W_BRANCH, tn), lambda i, j: (layer, 0, 0, j))],
        out_specs=pl.BlockSpec((tm, tn), lambda i, j: (i, j)),
        out_shape=jax.ShapeDtypeStruct((m, D_MODEL), BF16),
        compiler_params=_cp("arbitrary", "arbitrary"),
        name="merge",
    )(xn, *outs, p["w_t"], p["w_t"], p["w_t"], p["w_t"], p["w_branch"])


def _out_kernel(m_ref, w_ref, x_ref, g_ref, o_ref):
    j = pl.program_id(1)
    tn = w_ref.shape[1]
    o_ref[:, pl.ds(pl.multiple_of(j * tn, tn), tn)] = _dot(m_ref[...], w_ref[...])

    @pl.when(j == pl.num_programs(1) - 1)
    def _():
        z = o_ref[...]
        inv = lax.rsqrt(jnp.mean(z * z, axis=-1, keepdims=True) + EPS)
        o_ref[...] = x_ref[...] + z * inv * g_ref[...]


def _out_call(merged, x, layer, p, tm, tn):
    m = x.shape[0]
    return pl.pallas_call(
        _out_kernel,
        grid=(m // tm, D_MODEL // tn),
        in_specs=[pl.BlockSpec((tm, D_MODEL), lambda i, j: (i, 0)),
                  pl.BlockSpec((None, D_MODEL, tn), lambda i, j: (layer, 0, j)),
                  pl.BlockSpec((tm, D_MODEL), lambda i, j: (i, 0)),
                  _vec_spec(D_MODEL, layer)],
        out_specs=pl.BlockSpec((tm, D_MODEL), lambda i, j: (i, 0)),
        out_shape=jax.ShapeDtypeStruct((m, D_MODEL), F32),
        compiler_params=_cp("arbitrary", "arbitrary"),
        name="out_proj",
    )(merged, p["w_out"], x, p["norm_post_g"])


def _block_diag(mats):
    *lead, gb, r, c = mats.shape
    eye = jnp.eye(gb, dtype=mats.dtype)
    out = mats[..., :, :, None, :] * eye[:, None, :, None]
    return out.reshape(*lead, gb * r, gb * c)


def _prep_params(w):
    depth = w["w_in"].shape[0]
    p = {}
    p["w_t"] = jnp.swapaxes(w["w_in"], 1, 2).astype(BF16)
    p["w_branch"] = w["w_branch"].astype(BF16)
    p["w_out"] = w["w_out"].astype(BF16)
    p["s5_w_glu"] = w["s5_w_glu"].astype(BF16)
    for k in ("norm_pre_g", "norm_post_g"):
        p[k] = w[k].reshape(depth, 1, D_MODEL)
    for k in ("conv_a_b", "ln_a_g", "ln_a_b", "s5_b_glu", "ssd_norm_g", "mlp_ln_g", "mlp_ln_b"):
        p[k] = w[k].reshape(depth, 1, W_BRANCH)
    p["conv_a_w"] = w["conv_a_w"]

    dg = depth * S5_GROUPS
    rep = lambda t: jnp.repeat(t.reshape(dg, S5_STATE), S5_GROUP, axis=1)
    step_x = jnp.broadcast_to(w["s5_log_step"].reshape(dg, 1), (dg, S5_STATE * S5_GROUP))
    prep = _s5_prep_call(rep(w["s5_a_re"]), rep(w["s5_a_im"]), step_x,
                         w["s5_b_re"].reshape(dg, -1), w["s5_b_im"].reshape(dg, -1))
    p["ab_re"] = prep[0][:, ::S5_GROUP].reshape(depth, 1, S5_LANES)
    p["ab_im"] = prep[1][:, ::S5_GROUP].reshape(depth, 1, S5_LANES)
    tb = lambda t: jnp.swapaxes(t.reshape(depth, S5_NB, S5_GB, S5_STATE, S5_GROUP), 3, 4)
    p["wb_re"] = jnp.concatenate([_block_diag(tb(prep[2 + 2 * k])) for k in range(S5_FOLD)], axis=2).astype(BF16)
    p["wb_im"] = jnp.concatenate([_block_diag(tb(prep[3 + 2 * k])) for k in range(S5_FOLD)], axis=2).astype(BF16)
    tcm = lambda t: jnp.swapaxes(t.reshape(depth, S5_NB, S5_GB, S5_GROUP, S5_STATE), 3, 4)
    p["wc_re"] = _block_diag(tcm(w["s5_c_re"])).astype(BF16)
    p["wc_im"] = _block_diag(tcm(w["s5_c_im"])).astype(BF16)
    p["s5_d"] = w["s5_d"].reshape(depth, 1, W_BRANCH)

    p["ssd_conv_w"] = w["ssd_conv_w"]
    p["ssd_conv_b"] = w["ssd_conv_b"].reshape(depth, 1, SSD_CONV_DIM)
    padh = lambda t: jnp.pad(t.reshape(depth, 1, SSD_HEADS), ((0, 0), (0, 0), (0, DT_PAD - SSD_HEADS)))
    p["ssd_dt_bias"] = padh(w["ssd_dt_bias"])
    p["ssd_a_log"] = padh(w["ssd_a_log"])
    p["ssd_d"] = jnp.repeat(w["ssd_d"], W_BRANCH // SSD_HEADS, axis=1).reshape(depth, 1, W_BRANCH)
    r = jnp.arange(DT_PAD)[:, None]
    c = jnp.arange(W_BRANCH)[None, :] // (W_BRANCH // SSD_HEADS)
    p["head_expand"] = (r == c).astype(BF16)

    p["mlp_w_s"] = w["mlp_w_s"]
    p["mlp_b_s_t"] = jnp.pad(jnp.swapaxes(w["mlp_b_s"], 1, 2), ((0, 0), (0, 0), (0, LANE - MLP_GROUPS)))
    p["mlp_w0"] = jnp.repeat(w["mlp_w_s"][:, :, 0, 0], MLP_GD, axis=1).reshape(depth, 1, W_BRANCH)
    p["mlp_b0"] = jnp.repeat(w["mlp_b_s"][:, :, 0], MLP_GD, axis=1).reshape(depth, 1, W_BRANCH)
    return p


def _dense_head(x, layer, p, tm_norm, tm, tn):
    xn, dtraw = _norm_call(x, p["norm_pre_g"], p["w_t"], layer, tm_norm)
    proj_a = _inproj_call(xn, p["w_t"], layer, 0, WA_BRANCH_COLS, tm, tn)
    proj_d = _inproj_call(xn, p["w_t"], layer, WD_START, WD_BRANCH_COLS, tm, tn)
    return xn, dtraw, proj_a, proj_d


def _dense_tail(x, xn, outs, layer, p, tm_merge, tn_merge, tm_out, tn_out):
    merged = _merge_call(xn, outs, layer, p, tm_merge, tn_merge)
    return _out_call(merged, x, layer, p, tm_out, tn_out)


def _prompt_layer(x, n_seq, seq, layer, p):
    xn, dtraw, proj_a, proj_d = _dense_head(x, layer, p, 256, 1024, 1024)
    out_a, conv_a = _conva_prompt_call(proj_a, n_seq, seq, layer, p)
    y_b, s_re, s_im = _s5_prompt_call(proj_a, n_seq, seq, layer, p)
    out_b = _s5_glu_call(y_b, proj_a, layer, p, 512)
    out_c, ssd_st, conv_ssd = _ssd_prompt_call(proj_a, dtraw, n_seq, seq, layer, p)
    out_d = _mlp_prompt_call(proj_d, n_seq, seq, layer, p)
    y = _dense_tail(x, xn, (out_a, out_b, out_c, out_d), layer, p, 512, 256, 512, 512)
    return y, conv_a, s_re, s_im, ssd_st, conv_ssd


def _sample_layer(x, cache_a_t, h_re, h_im, h_ssd, buf_ssd_t, layer, p):
    ns = x.shape[0]
    xn, dtraw, proj_a, proj_d = _dense_head(x, layer, p, ns, ns, 1024)
    out_a, conv_a_t = _conva_sample_call(proj_a, cache_a_t, layer, p)
    y_b, s_re, s_im = _s5_sample_call(proj_a, h_re, h_im, layer, p)
    out_b = _s5_glu_call(y_b, proj_a, layer, p, ns)
    out_c, ssd_st, conv_ssd_t = _ssd_sample_call(proj_a, dtraw, buf_ssd_t, h_ssd, layer, p)
    out_d, v_d = _mlp_sample_call(proj_d, layer, p)
    y = _dense_tail(x, xn, (out_a, out_b, out_c, out_d), layer, p, ns, 256, ns, 512)
    return y, conv_a_t, s_re, s_im, ssd_st, conv_ssd_t, v_d


def kernel(x_prompt, x_sample, cache_conv_a, state_s5_re, state_s5_im, state_ssd, cache_conv_ssd, norm_pre_g, w_in, conv_a_w, conv_a_b, ln_a_g, ln_a_b, s5_a_re, s5_a_im, s5_log_step, s5_b_re, s5_b_im, s5_c_re, s5_c_im, s5_d, s5_w_glu, s5_b_glu, ssd_conv_w, ssd_conv_b, ssd_dt_bias, ssd_a_log, ssd_d, ssd_norm_g, mlp_ln_g, mlp_ln_b, mlp_w_s, mlp_b_s, w_branch, w_out, norm_post_g):
    w = dict(norm_pre_g=norm_pre_g, w_in=w_in, conv_a_w=conv_a_w, conv_a_b=conv_a_b, ln_a_g=ln_a_g,
             ln_a_b=ln_a_b, s5_a_re=s5_a_re, s5_a_im=s5_a_im, s5_log_step=s5_log_step, s5_b_re=s5_b_re,
             s5_b_im=s5_b_im, s5_c_re=s5_c_re, s5_c_im=s5_c_im, s5_d=s5_d, s5_w_glu=s5_w_glu,
             s5_b_glu=s5_b_glu, ssd_conv_w=ssd_conv_w, ssd_conv_b=ssd_conv_b, ssd_dt_bias=ssd_dt_bias,
             ssd_a_log=ssd_a_log, ssd_d=ssd_d, ssd_norm_g=ssd_norm_g, mlp_ln_g=mlp_ln_g, mlp_ln_b=mlp_ln_b,
             mlp_w_s=mlp_w_s, mlp_b_s=mlp_b_s, w_branch=w_branch, w_out=w_out, norm_post_g=norm_post_g)
    depth = w_in.shape[0]
    n_seq, seq, _ = x_prompt.shape
    ns = x_sample.shape[0]
    p = _prep_params(w)
    hp = x_prompt.reshape(n_seq * seq, D_MODEL)
    hs = x_sample.reshape(ns, D_MODEL)
    cache_a_t = jnp.swapaxes(cache_conv_a, 1, 2)
    buf_ssd_t = jnp.swapaxes(cache_conv_ssd, 1, 2)
    h_re = state_s5_re.reshape(depth, ns, S5_LANES)
    h_im = state_s5_im.reshape(depth, ns, S5_LANES)
    h_ssd = state_ssd.reshape(depth, ns, SSD_GROUPS, SSD_GW, SSD_STATE)
    acc = [[] for _ in range(11)]
    for i in range(depth):
        hp, a1, r1, m1, s1, c1 = _prompt_layer(hp, n_seq, seq, i, p)
        hs, a2, r2, m2, s2, c2, v2 = _sample_layer(hs, cache_a_t, h_re, h_im, h_ssd, buf_ssd_t, i, p)
        vals = (a1, a2,
                r1.reshape(n_seq, S5_GROUPS, S5_STATE), r2.reshape(ns, S5_GROUPS, S5_STATE),
                m1.reshape(n_seq, S5_GROUPS, S5_STATE), m2.reshape(ns, S5_GROUPS, S5_STATE),
                s1.reshape(n_seq, SSD_GROUPS, 4, 64, SSD_STATE), s2.reshape(ns, SSD_GROUPS, 4, 64, SSD_STATE),
                c1, c2, v2.reshape(ns, 1, W_BRANCH))
        for lst, v in zip(acc, vals):
            lst.append(v)
    st = [jnp.stack(l) for l in acc]
    st[1] = jnp.swapaxes(st[1], 1, 2)
    st[9] = jnp.swapaxes(st[9], 1, 2)
    return (hp.reshape(n_seq, seq, D_MODEL), hs.reshape(ns, 1, D_MODEL)) + tuple(st)
```

```python
import jax
import jax.numpy as jnp
from jax import lax
from jax.experimental import pallas as pl
from jax.experimental.pallas import tpu as pltpu

F32 = jnp.float32
BF16 = jnp.bfloat16
EPS = 1e-6

D_MODEL = 4096
W_BRANCH = 2048
N_BRANCH = 4
LANE = 128
CONV_A_WIDTH = 31
S5_GROUPS = 128
S5_GROUP = 16
S5_STATE = 64
S5_LANES = S5_GROUPS * S5_STATE
S5_GB = 8
S5_NB = S5_GROUPS // S5_GB
S5_BW = S5_GB * S5_STATE
S5_UW = S5_GB * S5_GROUP
S5_FOLD = 4
SSD_HEADS = 32
SSD_GROUPS = 8
SSD_STATE = 128
SSD_CHUNK = 128
SSD_CONV_DIM = 4096
SSD_GW = W_BRANCH // SSD_GROUPS
SSD_B_OFF = W_BRANCH
SSD_C_OFF = W_BRANCH + SSD_GROUPS * SSD_STATE
MLP_CHUNK = 128
MLP_GROUPS = 16
MLP_GD = W_BRANCH // MLP_GROUPS
DT_PAD = LANE
WA_BRANCH_COLS = 16384
WD_START = WA_BRANCH_COLS + SSD_HEADS
WD_BRANCH_COLS = 3 * W_BRANCH
WM_START = WD_START + WD_BRANCH_COLS
COL_A_VAL, COL_A_G, COL_A_GATE, COL_B_U, COL_B_GATE, COL_C_Z = 0, 1, 2, 3, 4, 5
COL_XBC4096 = 3
COL_D_U, COL_D_V, COL_D_GATE = 0, 1, 2

VMEM_LIMIT = 56 * 1024 * 1024
VMEM_LIMIT_RESIDENT = 60 * 1024 * 1024


def _cp(*sem, vmem=VMEM_LIMIT):
    return pltpu.CompilerParams(dimension_semantics=sem, vmem_limit_bytes=vmem)


def _vec_spec(width, layer):
    return pl.BlockSpec((None, 1, width), lambda *_: (layer, 0, 0))


def _layer_spec(shape, layer):
    return pl.BlockSpec((None,) + tuple(shape), lambda *_: (layer,) + (0,) * len(shape))


def _sigmoid(x):
    return jax.nn.sigmoid(x)


def _silu(x):
    return x * jax.nn.sigmoid(x)


def _gelu(x):
    return jax.nn.gelu(x)


def _dot(a, b):
    return jnp.dot(a, b, preferred_element_type=F32)


def _dot_nt(a, b):
    return lax.dot_general(a, b, (((1,), (1,)), ((), ())), preferred_element_type=F32)


def _split3(x):
    hi = x.astype(BF16)
    r1 = x - hi.astype(F32)
    mid = r1.astype(BF16)
    lo = (r1 - mid.astype(F32)).astype(BF16)
    return hi, mid, lo


def _dot_exact_rhs(a_bf16, x):
    hi, mid, lo = _split3(x)
    return _dot(a_bf16, hi) + _dot(a_bf16, mid) + _dot(a_bf16, lo)


def _dot_exact_lhs(x, e_bf16):
    hi, mid, lo = _split3(x)
    return _dot(hi, e_bf16) + _dot(mid, e_bf16) + _dot(lo, e_bf16)


def _norm_kernel(x_ref, g_ref, wdt_ref, xn_ref, dt_ref):
    x = x_ref[...]
    y = x * lax.rsqrt(jnp.mean(x * x, axis=-1, keepdims=True) + EPS) * g_ref[...]
    xb = y.astype(BF16)
    xn_ref[...] = xb
    d = _dot_nt(xb, wdt_ref[...])
    lane = lax.broadcasted_iota(jnp.int32, d.shape, 1)
    dt_ref[...] = jnp.where(lane < SSD_HEADS, d, 0.0)


def _norm_call(x, g, w_t, layer, tm):
    m = x.shape[0]
    return pl.pallas_call(
        _norm_kernel,
        grid=(m // tm,),
        in_specs=[pl.BlockSpec((tm, D_MODEL), lambda i: (i, 0)),
                  _vec_spec(D_MODEL, layer),
                  pl.BlockSpec((None, DT_PAD, D_MODEL), lambda i: (layer, WA_BRANCH_COLS // DT_PAD, 0))],
        out_specs=[pl.BlockSpec((tm, D_MODEL), lambda i: (i, 0)),
                   pl.BlockSpec((tm, DT_PAD), lambda i: (i, 0))],
        out_shape=[jax.ShapeDtypeStruct((m, D_MODEL), BF16),
                   jax.ShapeDtypeStruct((m, DT_PAD), F32)],
        compiler_params=_cp("arbitrary"),
        name="norm_dt",
    )(x, g, w_t)


W_ROW_ALIGN = 32


def _wt_spec(layer, row0, tn):
    return pl.BlockSpec((pl.Element(1), pl.Element(tn), pl.Element(D_MODEL)),
                        lambda i, j: (layer, pl.multiple_of(row0 + j * tn, W_ROW_ALIGN), 0))


def _mm_kernel(x_ref, w_ref, o_ref):
    o_ref[...] = _dot_nt(x_ref[...], w_ref[0]).astype(o_ref.dtype)


def _inproj_call(xn, w_t, layer, row0, n_cols, tm, tn):
    m, k = xn.shape
    return pl.pallas_call(
        _mm_kernel,
        grid=(m // tm, n_cols // tn),
        in_specs=[pl.BlockSpec((tm, k), lambda i, j: (i, 0)),
                  _wt_spec(layer, row0, tn)],
        out_specs=pl.BlockSpec((tm, tn), lambda i, j: (i, j)),
        out_shape=jax.ShapeDtypeStruct((m, n_cols), F32),
        compiler_params=_cp("arbitrary", "arbitrary"),
        name="in_proj",
    )(xn, w_t)


CA_TC = 128
CA_HALO = 32
CA_RB = 64
W_TILES = W_BRANCH // LANE


def _ln_swish_gate(v, lg, lb, gate):
    mu = jnp.mean(v, axis=-1, keepdims=True)
    xc = v - mu
    y = xc * lax.rsqrt(jnp.mean(xc * xc, axis=-1, keepdims=True) + EPS) * lg + lb
    return _silu(y) * _silu(gate)


def _conva_prompt_kernel(av_ref, ag_ref, gate_ref, w_ref, b_ref, lg_ref, lb_ref,
                         o_ref, nc_ref, ext_ref, cv_ref):
    c = pl.program_id(1)
    tc = CA_TC
    off = CA_HALO - (CONV_A_WIDTH - 1)

    @pl.when(c == 0)
    def _():
        ext_ref[:, 0:CA_HALO, :] = jnp.zeros((W_TILES, CA_HALO, LANE), F32)

    for lt in range(W_TILES):
        ls = slice(lt * LANE, (lt + 1) * LANE)
        ext_ref[lt, CA_HALO:CA_HALO + tc, :] = av_ref[:, ls] * _sigmoid(ag_ref[:, ls])

    def conv_tile(lt, carry):
        ls = pl.ds(pl.multiple_of(lt * LANE, LANE), LANE)
        for r0 in range(0, tc, CA_RB):
            accs = [jnp.broadcast_to(b_ref[:, ls], (8, LANE))] * (CA_RB // 8)
            for k in range(CONV_A_WIDTH):
                wk = jnp.broadcast_to(w_ref[k:k + 1, ls], (8, LANE))
                accs = [a + wk * ext_ref[lt, r0 + off + k + 8 * i:r0 + off + k + 8 * i + 8, :]
                        for i, a in enumerate(accs)]
            for i, a in enumerate(accs):
                cv_ref[r0 + 8 * i:r0 + 8 * i + 8, ls] = a
        return carry

    lax.fori_loop(0, W_TILES, conv_tile, 0)

    o_ref[...] = _ln_swish_gate(cv_ref[...], lg_ref[...], lb_ref[...], gate_ref[...]).astype(o_ref.dtype)

    @pl.when(c == pl.num_programs(1) - 1)
    def _():
        for lt in range(W_TILES):
            nc_ref[:, lt * LANE:(lt + 1) * LANE] = ext_ref[lt, tc + off:tc + CA_HALO, :]

    for lt in range(W_TILES):
        ext_ref[lt, 0:CA_HALO, :] = ext_ref[lt, tc:tc + CA_HALO, :]


def _conva_prompt_call(proj, n_seq, seq, layer, p):
    nck = seq // CA_TC
    row = lambda n, c: n * nck + c
    vec = _vec_spec(W_BRANCH, layer)
    return pl.pallas_call(
        _conva_prompt_kernel,
        grid=(n_seq, nck),
        in_specs=[pl.BlockSpec((CA_TC, W_BRANCH), lambda n, c: (row(n, c), COL_A_VAL)),
                  pl.BlockSpec((CA_TC, W_BRANCH), lambda n, c: (row(n, c), COL_A_G)),
                  pl.BlockSpec((CA_TC, W_BRANCH), lambda n, c: (row(n, c), COL_A_GATE)),
                  _layer_spec((CONV_A_WIDTH, W_BRANCH), layer),
                  vec, vec, vec],
        out_specs=[pl.BlockSpec((CA_TC, W_BRANCH), lambda n, c: (row(n, c), 0)),
                   pl.BlockSpec((None, CONV_A_WIDTH - 1, W_BRANCH), lambda n, c: (n, 0, 0))],
        out_shape=[jax.ShapeDtypeStruct((n_seq * seq, W_BRANCH), BF16),
                   jax.ShapeDtypeStruct((n_seq, CONV_A_WIDTH - 1, W_BRANCH), F32)],
        scratch_shapes=[pltpu.VMEM((W_TILES, CA_HALO + CA_TC, LANE), F32),
                        pltpu.VMEM((CA_TC, W_BRANCH), F32)],
        compiler_params=_cp("arbitrary", "arbitrary"),
        name="conv_a_prompt",
    )(proj, proj, proj, p["conv_a_w"], p["conv_a_b"], p["ln_a_g"], p["ln_a_b"])


SAMPLE_NB = 8


def _conva_sample_kernel(av_ref, ag_ref, gate_ref, cache_ref, w_ref, b_ref, lg_ref, lb_ref,
                         o_ref, nc_ref):
    h = av_ref[...] * _sigmoid(ag_ref[...])
    kw = CONV_A_WIDTH - 1
    acc = b_ref[...] + w_ref[kw:kw + 1, :] * h
    for k in range(kw):
        acc = acc + w_ref[k:k + 1, :] * cache_ref[k]
    o_ref[...] = _ln_swish_gate(acc, lg_ref[...], lb_ref[...], gate_ref[...]).astype(o_ref.dtype)
    for k in range(kw - 1):
        nc_ref[k] = cache_ref[k + 1]
    nc_ref[kw - 1] = h


def _conva_sample_call(proj, cache_t, layer, p):
    ns = proj.shape[0]
    nb = SAMPLE_NB
    kw = CONV_A_WIDTH - 1
    vec = _vec_spec(W_BRANCH, layer)
    return pl.pallas_call(
        _conva_sample_kernel,
        grid=(ns // nb,),
        in_specs=[pl.BlockSpec((nb, W_BRANCH), lambda i: (i, COL_A_VAL)),
                  pl.BlockSpec((nb, W_BRANCH), lambda i: (i, COL_A_G)),
                  pl.BlockSpec((nb, W_BRANCH), lambda i: (i, COL_A_GATE)),
                  pl.BlockSpec((None, kw, nb, W_BRANCH), lambda i: (layer, 0, i, 0)),
                  _layer_spec((CONV_A_WIDTH, W_BRANCH), layer),
                  vec, vec, vec],
        out_specs=[pl.BlockSpec((nb, W_BRANCH), lambda i: (i, 0)),
                   pl.BlockSpec((kw, nb, W_BRANCH), lambda i: (0, i, 0))],
        out_shape=[jax.ShapeDtypeStruct((ns, W_BRANCH), BF16),
                   jax.ShapeDtypeStruct((kw, ns, W_BRANCH), F32)],
        compiler_params=_cp("arbitrary"),
        name="conv_a_sample",
    )(proj, proj, proj, cache_t, p["conv_a_w"], p["conv_a_b"], p["ln_a_g"], p["ln_a_b"])


def _cmul(ar, ai, br, bi):
    return ar * br - ai * bi, ar * bi + ai * br


def _s5_prep_kernel(are_ref, aim_ref, ls_ref, bre_ref, bim_ref, abre_ref, abim_ref, *bb_refs):
    lam_re = are_ref[...]
    lam_im = aim_ref[...]
    step = jnp.exp(ls_ref[...])
    mag = jnp.exp(lam_re * step)
    ab_re = mag * jnp.cos(lam_im * step)
    ab_im = mag * jnp.sin(lam_im * step)
    den = lam_re * lam_re + lam_im * lam_im
    nr = ab_re - 1.0
    coef_re = (nr * lam_re + ab_im * lam_im) / den
    coef_im = (ab_im * lam_re - nr * lam_im) / den
    abre_ref[...] = ab_re
    abim_ref[...] = ab_im
    cur_re, cur_im = _cmul(coef_re, coef_im, bre_ref[...], bim_ref[...])
    for k in range(S5_FOLD):
        bb_refs[2 * k][...] = cur_re
        bb_refs[2 * k + 1][...] = cur_im
        cur_re, cur_im = _cmul(ab_re, ab_im, cur_re, cur_im)


def _s5_prep_call(a_re, a_im, log_step, b_re, b_im):
    rows = a_re.shape[0]
    rb = 1024
    col = pl.BlockSpec((rb, 1), lambda i: (i, 0))
    mat = pl.BlockSpec((rb, S5_GROUP), lambda i: (i, 0))
    col_sh = jax.ShapeDtypeStruct((rows, 1), F32)
    mat_sh = jax.ShapeDtypeStruct((rows, S5_GROUP), F32)
    return pl.pallas_call(
        _s5_prep_kernel,
        grid=(rows // rb,),
        in_specs=[col, col, col, mat, mat],
        out_specs=[col, col] + [mat] * (2 * S5_FOLD),
        out_shape=[col_sh, col_sh] + [mat_sh] * (2 * S5_FOLD),
        compiler_params=_cp("arbitrary"),
        name="s5_prep",
    )(a_re, a_im, log_step, b_re, b_im)


S5_TC = 1024
S5_UHEAD = 8
S5_SUB = 1024


def _s5_prompt_kernel(u_ref, abr_ref, abi_ref, wbr_ref, wbi_ref, wcr_ref, wci_ref, d_ref,
                      y_ref, sre_ref, sim_ref,
                      ush, hre, him, l4r, l4i, pwr, pwi, car, cai):
    c = pl.program_id(2)
    tc = S5_TC
    bw = S5_BW

    @pl.when(c == 0)
    def _():
        row = lax.broadcasted_iota(jnp.int32, (8, bw), 0)
        a1r = jnp.broadcast_to(abr_ref[...], (8, bw))
        a1i = jnp.broadcast_to(abi_ref[...], (8, bw))
        cr, ci = a1r, a1i
        pr = jnp.zeros((8, bw), F32)
        pi = jnp.zeros((8, bw), F32)
        for r in range(8):
            pr = jnp.where(row == r, cr, pr)
            pi = jnp.where(row == r, ci, pi)
            if r + 1 == S5_FOLD:
                l4r[...] = jnp.where(row >= S5_FOLD, cr, 0.0)
                l4i[...] = jnp.where(row >= S5_FOLD, ci, 0.0)
            cr, ci = _cmul(cr, ci, a1r, a1i)
        pwr[...] = pr
        pwi[...] = pi
        car[...] = jnp.zeros(car.shape, F32)
        cai[...] = jnp.zeros(cai.shape, F32)
        ush[0:S5_UHEAD, :] = jnp.zeros((S5_UHEAD, S5_UW), F32)

    ush[S5_UHEAD:S5_UHEAD + tc, :] = u_ref[...]
    sub = S5_SUB
    row8 = lax.broadcasted_iota(jnp.int32, (sub, S5_UW), 0) & 7
    cr = car[...]
    ci = cai[...]
    for r0 in range(0, tc, sub):
        u = u_ref[r0:r0 + sub, :]
        parts = [u.astype(BF16)]
        for k in range(1, S5_FOLD):
            uk = ush[S5_UHEAD - k + r0:S5_UHEAD - k + r0 + sub, :]
            parts.append(jnp.where(row8 >= k, uk, 0.0).astype(BF16))
        lhs = jnp.concatenate(parts, axis=-1)
        hre[r0:r0 + sub, :] = _dot_nt(lhs, wbr_ref[...])
        him[r0:r0 + sub, :] = _dot_nt(lhs, wbi_ref[...])
        for t0 in range(r0, r0 + sub, 8):
            xr = hre[t0:t0 + 8, :]
            xi = him[t0:t0 + 8, :]
            tr, ti = _cmul(l4r[...], l4i[...], pltpu.roll(xr, S5_FOLD, 0), pltpu.roll(xi, S5_FOLD, 0))
            xr = xr + tr
            xi = xi + ti
            tr, ti = _cmul(pwr[...], pwi[...], cr, ci)
            xr = xr + tr
            xi = xi + ti
            hre[t0:t0 + 8, :] = xr
            him[t0:t0 + 8, :] = xi
            cr = jnp.broadcast_to(xr[7:8, :], (8, bw))
            ci = jnp.broadcast_to(xi[7:8, :], (8, bw))
        y_ref[r0:r0 + sub, :] = (_dot_nt(hre[r0:r0 + sub, :].astype(BF16), wcr_ref[...])
                                 - _dot_nt(him[r0:r0 + sub, :].astype(BF16), wci_ref[...])
                                 + d_ref[...] * u)
    car[...] = cr
    cai[...] = ci

    @pl.when(c == pl.num_programs(2) - 1)
    def _():
        sre_ref[...] = cr[0:1, :]
        sim_ref[...] = ci[0:1, :]


def _s5_prompt_call(proj, n_seq, seq, layer, p):
    nck = seq // S5_TC
    ub = W_BRANCH // S5_UW
    lane_blk = lambda w: pl.BlockSpec((None, 1, w), lambda n, j, c: (layer, 0, j))
    wspec = lambda r, cc: pl.BlockSpec((None, None, r, cc), lambda n, j, c: (layer, j, 0, 0))
    sspec = pl.BlockSpec((None, 1, S5_BW), lambda n, j, c: (n, 0, j))
    return pl.pallas_call(
        _s5_prompt_kernel,
        grid=(n_seq, S5_NB, nck),
        in_specs=[pl.BlockSpec((S5_TC, S5_UW), lambda n, j, c: (n * nck + c, COL_B_U * ub + j)),
                  lane_blk(S5_BW), lane_blk(S5_BW),
                  wspec(S5_BW, S5_FOLD * S5_UW), wspec(S5_BW, S5_FOLD * S5_UW),
                  wspec(S5_UW, S5_BW), wspec(S5_UW, S5_BW),
                  lane_blk(S5_UW)],
        out_specs=[pl.BlockSpec((S5_TC, S5_UW), lambda n, j, c: (n * nck + c, j)), sspec, sspec],
        out_shape=[jax.ShapeDtypeStruct((n_seq * seq, W_BRANCH), F32),
                   jax.ShapeDtypeStruct((n_seq, 1, S5_LANES), F32),
                   jax.ShapeDtypeStruct((n_seq, 1, S5_LANES), F32)],
        scratch_shapes=[pltpu.VMEM((S5_UHEAD + S5_TC, S5_UW), F32),
                        pltpu.VMEM((S5_TC, S5_BW), F32), pltpu.VMEM((S5_TC, S5_BW), F32)]
        + [pltpu.VMEM((8, S5_BW), F32)] * 6,
        compiler_params=_cp("arbitrary", "arbitrary", "arbitrary"),
        name="s5_prompt",
    )(proj, p["ab_re"], p["ab_im"], p["wb_re"], p["wb_im"], p["wc_re"], p["wc_im"], p["s5_d"])


def _s5_sample_kernel(u_ref, h0r_ref, h0i_ref, abr_ref, abi_ref, wbr_ref, wbi_ref, wcr_ref, wci_ref,
                      d_ref, y_ref, sre_ref, sim_ref):
    for j in range(S5_NB):
        us = slice(j * S5_UW, (j + 1) * S5_UW)
        ss = slice(j * S5_BW, (j + 1) * S5_BW)
        ub = u_ref[:, us].astype(BF16)
        ar = abr_ref[:, ss]
        ai = abi_ref[:, ss]
        h0r = h0r_ref[:, ss]
        h0i = h0i_ref[:, ss]
        nr = ar * h0r - ai * h0i + _dot_nt(ub, wbr_ref[j])
        ni = ar * h0i + ai * h0r + _dot_nt(ub, wbi_ref[j])
        sre_ref[:, ss] = nr
        sim_ref[:, ss] = ni
        y_ref[:, us] = (_dot_nt(nr.astype(BF16), wcr_ref[j]) - _dot_nt(ni.astype(BF16), wci_ref[j])
                        + d_ref[:, us] * u_ref[:, us])


def _s5_sample_call(proj, h0_re, h0_im, layer, p):
    ns = proj.shape[0]
    tb = 64
    h0spec = pl.BlockSpec((None, tb, S5_LANES), lambda i: (layer, i, 0))
    sspec = pl.BlockSpec((tb, S5_LANES), lambda i: (i, 0))
    wb = pl.BlockSpec((None, S5_NB, S5_BW, S5_UW), lambda i: (layer, 0, 0, 0))
    wc = pl.BlockSpec((None, S5_NB, S5_UW, S5_BW), lambda i: (layer, 0, 0, 0))
    return pl.pallas_call(
        _s5_sample_kernel,
        grid=(ns // tb,),
        in_specs=[pl.BlockSpec((tb, W_BRANCH), lambda i: (i, COL_B_U)), h0spec, h0spec,
                  _vec_spec(S5_LANES, layer), _vec_spec(S5_LANES, layer), wb, wb, wc, wc,
                  _vec_spec(W_BRANCH, layer)],
        out_specs=[pl.BlockSpec((tb, W_BRANCH), lambda i: (i, 0)), sspec, sspec],
        out_shape=[jax.ShapeDtypeStruct((ns, W_BRANCH), F32),
                   jax.ShapeDtypeStruct((ns, S5_LANES), F32),
                   jax.ShapeDtypeStruct((ns, S5_LANES), F32)],
        compiler_params=_cp("arbitrary"),
        name="s5_sample",
    )(proj, h0_re, h0_im, p["ab_re"], p["ab_im"], p["wb_re"], p["wb_im"], p["wc_re"], p["wc_im"], p["s5_d"])


def _s5_glu_kernel(y_ref, gate_ref, w_ref, b_ref, o_ref):
    g = _gelu(y_ref[...])
    z = _dot(g.astype(BF16), w_ref[...]) + b_ref[...]
    o_ref[...] = (g * _sigmoid(z) * _silu(gate_ref[...])).astype(o_ref.dtype)


def _s5_glu_call(y, proj, layer, p, tm):
    m = y.shape[0]
    return pl.pallas_call(
        _s5_glu_kernel,
        grid=(m // tm,),
        in_specs=[pl.BlockSpec((tm, W_BRANCH), lambda i: (i, 0)),
                  pl.BlockSpec((tm, W_BRANCH), lambda i: (i, COL_B_GATE)),
                  _layer_spec((W_BRANCH, W_BRANCH), layer),
                  _vec_spec(W_BRANCH, layer)],
        out_specs=pl.BlockSpec((tm, W_BRANCH), lambda i: (i, 0)),
        out_shape=jax.ShapeDtypeStruct((m, W_BRANCH), BF16),
        compiler_params=_cp("arbitrary"),
        name="s5_glu",
    )(y, proj, p["s5_w_glu"], p["s5_b_glu"])


SSD_EXT_HEAD = 8
SSD_TAPS = 4
XBC_TILES = SSD_CONV_DIM // LANE


def _softplus(x):
    return jnp.maximum(x, 0.0) + jnp.log(1.0 + jnp.exp(-jnp.abs(x)))


def _ssd_prompt_kernel(z_ref, xbc_ref, dtr_ref, cw_ref, cb_ref, dtb_ref, alog_ref, dvec_ref, ng_ref,
                       o_ref, st_ref, nb_ref,
                       ext, h, xa, ybuf):
    c = pl.program_id(1)
    q = SSD_CHUNK
    e0 = SSD_EXT_HEAD
    off = e0 - (SSD_TAPS - 1)

    @pl.when(c == 0)
    def _():
        ext[:, 0:e0, :] = jnp.zeros((XBC_TILES, e0, LANE), F32)
        h[...] = jnp.zeros(h.shape, F32)

    rb = 64

    def conv_tile(lt, carry):
        ls = pl.ds(pl.multiple_of(lt * LANE, LANE), LANE)
        ext[lt, e0:e0 + q, :] = xbc_ref[:, ls]
        for r0 in range(0, q, rb):
            accs = [jnp.broadcast_to(cb_ref[:, ls], (8, LANE))] * (rb // 8)
            for k in range(SSD_TAPS):
                wk = jnp.broadcast_to(cw_ref[k:k + 1, ls], (8, LANE))
                accs = [a + wk * ext[lt, r0 + off + k + 8 * i:r0 + off + k + 8 * i + 8, :]
                        for i, a in enumerate(accs)]
            for i, a in enumerate(accs):
                xa[r0 + 8 * i:r0 + 8 * i + 8, ls] = _silu(a)
        return carry

    lax.fori_loop(0, XBC_TILES, conv_tile, 0)

    @pl.when(c == pl.num_programs(1) - 1)
    def _():
        for lt in range(XBC_TILES):
            nb_ref[:, lt * LANE:(lt + 1) * LANE] = ext[lt, q + off:q + e0, :]

    for lt in range(XBC_TILES):
        ext[lt, 0:e0, :] = ext[lt, q:q + e0, :]

    dt = _softplus(dtr_ref[...] + dtb_ref[...])
    a = -jnp.exp(alog_ref[...])
    da = dt * a
    row = lax.broadcasted_iota(jnp.int32, (q, q), 0)
    col = lax.broadcasted_iota(jnp.int32, (q, q), 1)
    tri = row >= col
    lt_ones = jnp.where(tri, 1.0, 0.0).astype(BF16)
    dacs = _dot_exact_rhs(lt_ones, da)
    dcs_t = dacs.T
    dt_t = dt.T
    tot_t = jnp.broadcast_to(dcs_t[:, q - 1:q], (q, q))
    w_t = dt_t * jnp.exp(tot_t - dcs_t)
    cd_t = jnp.exp(tot_t)
    lane_lo = col < 64
    row_lo = row < 64

    def rows_of(m, r):
        return jnp.broadcast_to(m[r:r + 1, :], (q, q))

    for g in range(SSD_GROUPS):
        bm_b = xa[:, SSD_B_OFF + g * SSD_STATE:SSD_B_OFF + (g + 1) * SSD_STATE].astype(BF16)
        cm_b = xa[:, SSD_C_OFF + g * SSD_STATE:SSD_C_OFF + (g + 1) * SSD_STATE].astype(BF16)
        cb = _dot_nt(cm_b, bm_b)
        for p in range(2):
            r_lo = 4 * g + 2 * p
            cs = slice(g * SSD_GW + p * LANE, g * SSD_GW + (p + 1) * LANE)
            xs = xa[:, cs]
            ydiag = None
            eo = []
            for hh in range(2):
                r = r_lo + hh
                a_col = jnp.broadcast_to(dacs[:, r:r + 1], (q, q))
                seg = a_col - rows_of(dcs_t, r)
                lmat = jnp.where(tri, jnp.exp(seg), 0.0)
                mp = cb * lmat * rows_of(dt_t, r)
                keep = lane_lo if hh == 0 else jnp.logical_not(lane_lo)
                xs_h = jnp.where(keep, xs, 0.0)
                term = _dot(mp.astype(BF16), xs_h.astype(BF16))
                ydiag = term if ydiag is None else ydiag + term
                eo.append(jnp.exp(a_col))
            eo_pair = jnp.where(lane_lo, eo[0], eo[1])
            hs = slice(p * LANE, (p + 1) * LANE)
            h_pair = h[g, hs, :]
            yoff = _dot_nt(cm_b, h_pair.astype(BF16)) * eo_pair
            ybuf[:, cs] = ydiag + yoff + dvec_ref[:, cs] * xs
            w_rows = jnp.where(row_lo, rows_of(w_t, r_lo), rows_of(w_t, r_lo + 1))
            cd_rows = jnp.where(row_lo, rows_of(cd_t, r_lo), rows_of(cd_t, r_lo + 1))
            st = _dot((xs.T * w_rows).astype(BF16), bm_b)
            h[g, hs, :] = h_pair * cd_rows + st

    y = ybuf[...] * _silu(z_ref[...])
    for g in range(SSD_GROUPS):
        gs = slice(g * SSD_GW, (g + 1) * SSD_GW)
        yg = y[:, gs]
        o_ref[:, gs] = (yg * lax.rsqrt(jnp.mean(yg * yg, axis=-1, keepdims=True) + EPS)
                        * ng_ref[:, gs]).astype(o_ref.dtype)

    @pl.when(c == pl.num_programs(1) - 1)
    def _():
        st_ref[...] = h[...]


def _ssd_prompt_call(proj, dtraw, n_seq, seq, layer, p):
    q = SSD_CHUNK
    nck = seq // q
    row = lambda n, c: n * nck + c
    return pl.pallas_call(
        _ssd_prompt_kernel,
        grid=(n_seq, nck),
        in_specs=[pl.BlockSpec((q, W_BRANCH), lambda n, c: (row(n, c), COL_C_Z)),
                  pl.BlockSpec((q, SSD_CONV_DIM), lambda n, c: (row(n, c), COL_XBC4096)),
                  pl.BlockSpec((q, DT_PAD), lambda n, c: (row(n, c), 0)),
                  _layer_spec((SSD_TAPS, SSD_CONV_DIM), layer),
                  _vec_spec(SSD_CONV_DIM, layer),
                  _vec_spec(DT_PAD, layer), _vec_spec(DT_PAD, layer),
                  _vec_spec(W_BRANCH, layer), _vec_spec(W_BRANCH, layer)],
        out_specs=[pl.BlockSpec((q, W_BRANCH), lambda n, c: (row(n, c), 0)),
                   pl.BlockSpec((None, SSD_GROUPS, SSD_GW, SSD_STATE), lambda n, c: (n, 0, 0, 0)),
                   pl.BlockSpec((None, SSD_TAPS - 1, SSD_CONV_DIM), lambda n, c: (n, 0, 0))],
        out_shape=[jax.ShapeDtypeStruct((n_seq * seq, W_BRANCH), BF16),
                   jax.ShapeDtypeStruct((n_seq, SSD_GROUPS, SSD_GW, SSD_STATE), F32),
                   jax.ShapeDtypeStruct((n_seq, SSD_TAPS - 1, SSD_CONV_DIM), F32)],
        scratch_shapes=[pltpu.VMEM((XBC_TILES, SSD_EXT_HEAD + q, LANE), F32),
                        pltpu.VMEM((SSD_GROUPS, SSD_GW, SSD_STATE), F32),
                        pltpu.VMEM((q, SSD_CONV_DIM), F32),
                        pltpu.VMEM((q, W_BRANCH), F32)],
        compiler_params=_cp("arbitrary", "arbitrary"),
        name="ssd_prompt",
    )(proj, proj, dtraw, p["ssd_conv_w"], p["ssd_conv_b"], p["ssd_dt_bias"], p["ssd_a_log"],
      p["ssd_d"], p["ssd_norm_g"])


def _ssd_sample_kernel(z_ref, xbc_ref, dtr_ref, buf_ref, h0_ref, cw_ref, cb_ref, dtb_ref, alog_ref,
                       dvec_ref, ng_ref, ex_ref,
                       o_ref, st_ref, nb_ref, colx, cold):
    nb = SAMPLE_NB
    xbc = xbc_ref[...]
    acc = cb_ref[...] + cw_ref[SSD_TAPS - 1:SSD_TAPS, :] * xbc
    for k in range(SSD_TAPS - 1):
        acc = acc + cw_ref[k:k + 1, :] * buf_ref[k]
    xa = _silu(acc)
    for k in range(SSD_TAPS - 2):
        nb_ref[k] = buf_ref[k + 1]
    nb_ref[SSD_TAPS - 2] = xbc

    dt = _softplus(dtr_ref[...] + dtb_ref[...])
    a = -jnp.exp(alog_ref[...])
    dec = jnp.exp(dt * a)
    ex = ex_ref[...]
    dt_x = _dot_exact_lhs(dt, ex)
    dec_x = _dot_exact_lhs(dec, ex)
    xs = xa[:, 0:W_BRANCH]
    xdt = xs * dt_x

    pad = jnp.zeros((LANE - nb, W_BRANCH), F32)
    colx[...] = jnp.concatenate([xdt, pad], axis=0).T
    cold[...] = jnp.concatenate([dec_x, pad], axis=0).T

    rown = lax.broadcasted_iota(jnp.int32, (nb, SSD_GW), 0)
    for g in range(SSD_GROUPS):
        gs = slice(g * SSD_GW, (g + 1) * SSD_GW)
        bm = xa[:, SSD_B_OFF + g * SSD_STATE:SSD_B_OFF + (g + 1) * SSD_STATE]
        cm = xa[:, SSD_C_OFF + g * SSD_STATE:SSD_C_OFF + (g + 1) * SSD_STATE]
        cm_b = cm.astype(BF16)
        cbv = jnp.sum(cm * bm, axis=-1, keepdims=True)
        yoff = jnp.zeros((nb, SSD_GW), F32)
        for n in range(nb):
            h0 = h0_ref[n, g]
            xcol = jnp.broadcast_to(colx[gs, n:n + 1], (SSD_GW, SSD_STATE))
            dcol = jnp.broadcast_to(cold[gs, n:n + 1], (SSD_GW, SSD_STATE))
            st_ref[n, g] = h0 * dcol + xcol * jnp.broadcast_to(bm[n:n + 1, :], (SSD_GW, SSD_STATE))
            yoff = yoff + jnp.where(rown == n, _dot_nt(cm_b, h0.astype(BF16)), 0.0)
        yg = (yoff * dec_x[:, gs] + cbv * xdt[:, gs] + dvec_ref[:, gs] * xs[:, gs]) * _silu(z_ref[:, gs])
        o_ref[:, gs] = (yg * lax.rsqrt(jnp.mean(yg * yg, axis=-1, keepdims=True) + EPS)
                        * ng_ref[:, gs]).astype(o_ref.dtype)


def _ssd_sample_kernel_onto(prev_ref, *refs):
    del prev_ref
    _ssd_sample_kernel(*refs)


def _ssd_sample_call(proj, dtraw, buf_t, h0, layer, p, state_stack):
    ns = proj.shape[0]
    nb = SAMPLE_NB
    nt = SSD_TAPS - 1
    depth = h0.shape[0]
    hshape = (nb, SSD_GROUPS, SSD_GW, SSD_STATE)
    onto = state_stack is not None
    return pl.pallas_call(
        _ssd_sample_kernel_onto if onto else _ssd_sample_kernel,
        grid=(ns // nb,),
        in_specs=([pl.BlockSpec(memory_space=pl.ANY)] if onto else [])
        + [pl.BlockSpec((nb, W_BRANCH), lambda i: (i, COL_C_Z)),
                  pl.BlockSpec((nb, SSD_CONV_DIM), lambda i: (i, COL_XBC4096)),
                  pl.BlockSpec((nb, DT_PAD), lambda i: (i, 0)),
                  pl.BlockSpec((None, nt, nb, SSD_CONV_DIM), lambda i: (layer, 0, i, 0)),
                  pl.BlockSpec((None,) + hshape, lambda i: (layer, i, 0, 0, 0)),
                  _layer_spec((SSD_TAPS, SSD_CONV_DIM), layer),
                  _vec_spec(SSD_CONV_DIM, layer),
                  _vec_spec(DT_PAD, layer), _vec_spec(DT_PAD, layer),
                  _vec_spec(W_BRANCH, layer), _vec_spec(W_BRANCH, layer),
                  pl.BlockSpec((DT_PAD, W_BRANCH), lambda i: (0, 0))],
        out_specs=[pl.BlockSpec((nb, W_BRANCH), lambda i: (i, 0)),
                   pl.BlockSpec((None,) + hshape, lambda i: (layer, i, 0, 0, 0)),
                   pl.BlockSpec((nt, nb, SSD_CONV_DIM), lambda i: (0, i, 0))],
        out_shape=[jax.ShapeDtypeStruct((ns, W_BRANCH), BF16),
                   jax.ShapeDtypeStruct((depth, ns, SSD_GROUPS, SSD_GW, SSD_STATE), F32),
                   jax.ShapeDtypeStruct((nt, ns, SSD_CONV_DIM), F32)],
        scratch_shapes=[pltpu.VMEM((W_BRANCH, LANE), F32), pltpu.VMEM((W_BRANCH, LANE), F32)],
        input_output_aliases={0: 1} if onto else {},
        compiler_params=_cp("arbitrary"),
        name="ssd_sample",
    )(*([state_stack] if onto else []), proj, proj, dtraw, buf_t, h0, p["ssd_conv_w"], p["ssd_conv_b"],
      p["ssd_dt_bias"], p["ssd_a_log"], p["ssd_d"], p["ssd_norm_g"], p["head_expand"])


def _layer_norm(x, g, b):
    mu = jnp.mean(x, axis=-1, keepdims=True)
    xc = x - mu
    return xc * lax.rsqrt(jnp.mean(xc * xc, axis=-1, keepdims=True) + EPS) * g + b


def _mlp_prompt_kernel(u_ref, v_ref, gate_ref, lg_ref, lb_ref, ws_ref, bst_ref, o_ref, vd):
    q = MLP_CHUNK
    vd[...] = _layer_norm(_gelu(v_ref[...]), lg_ref[...], lb_ref[...])
    row = lax.broadcasted_iota(jnp.int32, (q, q), 0)
    col = lax.broadcasted_iota(jnp.int32, (q, q), 1)
    tri = row >= col
    for g in range(MLP_GROUPS):
        gs = slice(g * MLP_GD, (g + 1) * MLP_GD)
        ws = jnp.where(tri, ws_ref[g], 0.0).astype(BF16)
        mixed = _dot(ws, vd[:, gs].astype(BF16)) + jnp.broadcast_to(bst_ref[:, g:g + 1], (q, MLP_GD))
        o_ref[:, gs] = (_gelu(u_ref[:, gs]) * mixed * _silu(gate_ref[:, gs])).astype(o_ref.dtype)


def _mlp_prompt_call(proj, n_seq, seq, layer, p):
    q = MLP_CHUNK
    nck = seq // q
    row = lambda n, c: n * nck + c
    return pl.pallas_call(
        _mlp_prompt_kernel,
        grid=(n_seq, nck),
        in_specs=[pl.BlockSpec((q, W_BRANCH), lambda n, c: (row(n, c), COL_D_U)),
                  pl.BlockSpec((q, W_BRANCH), lambda n, c: (row(n, c), COL_D_V)),
                  pl.BlockSpec((q, W_BRANCH), lambda n, c: (row(n, c), COL_D_GATE)),
                  _vec_spec(W_BRANCH, layer), _vec_spec(W_BRANCH, layer),
                  _layer_spec((MLP_GROUPS, q, q), layer),
                  _layer_spec((q, LANE), layer)],
        out_specs=pl.BlockSpec((q, W_BRANCH), lambda n, c: (row(n, c), 0)),
        out_shape=jax.ShapeDtypeStruct((n_seq * seq, W_BRANCH), BF16),
        scratch_shapes=[pltpu.VMEM((q, W_BRANCH), F32)],
        compiler_params=_cp("arbitrary", "arbitrary"),
        name="mlp_prompt",
    )(proj, proj, proj, p["mlp_ln_g"], p["mlp_ln_b"], p["mlp_w_s"], p["mlp_b_s_t"])


def _mlp_sample_kernel(u_ref, v_ref, gate_ref, lg_ref, lb_ref, w0_ref, b0_ref, o_ref, vd_ref):
    vd = _layer_norm(_gelu(v_ref[...]), lg_ref[...], lb_ref[...])
    vd_ref[...] = vd
    mixed = w0_ref[...] * vd + b0_ref[...]
    o_ref[...] = (_gelu(u_ref[...]) * mixed * _silu(gate_ref[...])).astype(o_ref.dtype)


def _mlp_sample_call(proj, layer, p):
    ns = proj.shape[0]
    tb = 64
    vec = _vec_spec(W_BRANCH, layer)
    return pl.pallas_call(
        _mlp_sample_kernel,
        grid=(ns // tb,),
        in_specs=[pl.BlockSpec((tb, W_BRANCH), lambda i: (i, COL_D_U)),
                  pl.BlockSpec((tb, W_BRANCH), lambda i: (i, COL_D_V)),
                  pl.BlockSpec((tb, W_BRANCH), lambda i: (i, COL_D_GATE)),
                  vec, vec, vec, vec],
        out_specs=[pl.BlockSpec((tb, W_BRANCH), lambda i: (i, 0)),
                   pl.BlockSpec((tb, W_BRANCH), lambda i: (i, 0))],
        out_shape=[jax.ShapeDtypeStruct((ns, W_BRANCH), BF16),
                   jax.ShapeDtypeStruct((ns, W_BRANCH), F32)],
        compiler_params=_cp("arbitrary"),
        name="mlp_sample",
    )(proj, proj, proj, p["mlp_ln_g"], p["mlp_ln_b"], p["mlp_w0"], p["mlp_b0"])


def _merge_kernel(xn_ref, oa_ref, ob_ref, oc_ref, od_ref,
                  wm0_ref, wm1_ref, wm2_ref, wm3_ref, wb_ref, o_ref):
    xn = xn_ref[...]
    acc = None
    for b, (o_b, wm) in enumerate(((oa_ref, wm0_ref), (ob_ref, wm1_ref), (oc_ref, wm2_ref), (od_ref, wm3_ref))):
        gate = _sigmoid(_dot_nt(xn, wm[0]))
        term = gate * _dot(o_b[...], wb_ref[b])
        acc = term if acc is None else acc + term
    o_ref[...] = acc.astype(o_ref.dtype)


def _merge_call(xn, outs, layer, p, tm, tn):
    m = xn.shape[0]
    nj = D_MODEL // tn

    def wm_spec(b):
        return _wt_spec(layer, WM_START + b * D_MODEL, tn)

    ospec = pl.BlockSpec((tm, W_BRANCH), lambda i, j: (i, 0))
    return pl.pallas_call(
        _merge_kernel,
        grid=(m // tm, nj),
        in_specs=[pl.BlockSpec((tm, D_MODEL), lambda i, j: (i, 0)), ospec, ospec, ospec, ospec,
                  wm_spec(0), wm_spec(1), wm_spec(2), wm_spec(3),
                  pl.BlockSpec((None, N_BRANCH, W_BRANCH, tn), lambda i, j: (layer, 0, 0, j))],
        out_specs=pl.BlockSpec((tm, tn), lambda i, j: (i, j)),
        out_shape=jax.ShapeDtypeStruct((m, D_MODEL), BF16),
        compiler_params=_cp("arbitrary", "arbitrary"),
        name="merge",
    )(xn, *outs, p["w_t"], p["w_t"], p["w_t"], p["w_t"], p["w_branch"])


def _out_kernel(m_ref, w_ref, x_ref, g_ref, o_ref):
    z = _dot(m_ref[...], w_ref[...])
    inv = lax.rsqrt(jnp.mean(z * z, axis=-1, keepdims=True) + EPS)
    o_ref[...] = x_ref[...] + z * inv * g_ref[...]


def _out_call(merged, x, layer, p, tm):
    m = x.shape[0]
    return pl.pallas_call(
        _out_kernel,
        grid=(m // tm,),
        in_specs=[pl.BlockSpec((tm, D_MODEL), lambda i: (i, 0)),
                  pl.BlockSpec((None, D_MODEL, D_MODEL), lambda i: (layer, 0, 0), pipeline_mode=pl.Buffered(1)),
                  pl.BlockSpec((tm, D_MODEL), lambda i: (i, 0)),
                  _vec_spec(D_MODEL, layer)],
        out_specs=pl.BlockSpec((tm, D_MODEL), lambda i: (i, 0)),
        out_shape=jax.ShapeDtypeStruct((m, D_MODEL), F32),
        compiler_params=_cp("arbitrary", vmem=VMEM_LIMIT_RESIDENT),
        name="out_proj",
    )(merged, p["w_out"], x, p["norm_post_g"])


def _block_diag(mats):
    gb, c = mats.shape[-3], mats.shape[-1]
    lead = [(0, 0)] * (mats.ndim - 2)
    rows = [jnp.pad(mats[..., g, :, :], lead + [(g * c, (gb - 1 - g) * c)]) for g in range(gb)]
    return jnp.concatenate(rows, axis=-2)


def _prep_params(w):
    depth = w["w_in"].shape[0]
    p = {}
    p["w_t"] = jnp.swapaxes(w["w_in"], 1, 2).astype(BF16)
    p["w_branch"] = w["w_branch"].astype(BF16)
    p["w_out"] = w["w_out"].astype(BF16)
    p["s5_w_glu"] = w["s5_w_glu"].astype(BF16)
    for k in ("norm_pre_g", "norm_post_g"):
        p[k] = w[k].reshape(depth, 1, D_MODEL)
    for k in ("conv_a_b", "ln_a_g", "ln_a_b", "s5_b_glu", "ssd_norm_g", "mlp_ln_g", "mlp_ln_b"):
        p[k] = w[k].reshape(depth, 1, W_BRANCH)
    p["conv_a_w"] = w["conv_a_w"]

    rows = depth * S5_GROUPS * S5_STATE
    prep = _s5_prep_call(w["s5_a_re"].reshape(rows, 1), w["s5_a_im"].reshape(rows, 1),
                         jnp.repeat(w["s5_log_step"].reshape(-1), S5_STATE).reshape(rows, 1),
                         w["s5_b_re"].reshape(rows, S5_GROUP), w["s5_b_im"].reshape(rows, S5_GROUP))
    p["ab_re"] = prep[0].reshape(depth, 1, S5_LANES)
    p["ab_im"] = prep[1].reshape(depth, 1, S5_LANES)
    blk = lambda t, r, c: _block_diag(t.reshape(depth, S5_NB, S5_GB, r, c))
    p["wb_re"] = jnp.concatenate([blk(prep[2 + 2 * k], S5_STATE, S5_GROUP) for k in range(S5_FOLD)], axis=3).astype(BF16)
    p["wb_im"] = jnp.concatenate([blk(prep[3 + 2 * k], S5_STATE, S5_GROUP) for k in range(S5_FOLD)], axis=3).astype(BF16)
    p["wc_re"] = blk(w["s5_c_re"], S5_GROUP, S5_STATE).astype(BF16)
    p["wc_im"] = blk(w["s5_c_im"], S5_GROUP, S5_STATE).astype(BF16)
    p["s5_d"] = w["s5_d"].reshape(depth, 1, W_BRANCH)

    p["ssd_conv_w"] = w["ssd_conv_w"]
    p["ssd_conv_b"] = w["ssd_conv_b"].reshape(depth, 1, SSD_CONV_DIM)
    padh = lambda t: jnp.pad(t.reshape(depth, 1, SSD_HEADS), ((0, 0), (0, 0), (0, DT_PAD - SSD_HEADS)))
    p["ssd_dt_bias"] = padh(w["ssd_dt_bias"])
    p["ssd_a_log"] = padh(w["ssd_a_log"])
    p["ssd_d"] = jnp.repeat(w["ssd_d"], W_BRANCH // SSD_HEADS, axis=1).reshape(depth, 1, W_BRANCH)
    r = jnp.arange(DT_PAD)[:, None]
    c = jnp.arange(W_BRANCH)[None, :] // (W_BRANCH // SSD_HEADS)
    p["head_expand"] = (r == c).astype(BF16)

    p["mlp_w_s"] = w["mlp_w_s"]
    p["mlp_b_s_t"] = jnp.pad(jnp.swapaxes(w["mlp_b_s"], 1, 2), ((0, 0), (0, 0), (0, LANE - MLP_GROUPS)))
    p["mlp_w0"] = jnp.repeat(w["mlp_w_s"][:, :, 0, 0], MLP_GD, axis=1).reshape(depth, 1, W_BRANCH)
    p["mlp_b0"] = jnp.repeat(w["mlp_b_s"][:, :, 0], MLP_GD, axis=1).reshape(depth, 1, W_BRANCH)
    return p


def _dense_head(x, layer, p, tm_norm, tm, tn):
    xn, dtraw = _norm_call(x, p["norm_pre_g"], p["w_t"], layer, tm_norm)
    proj_a = _inproj_call(xn, p["w_t"], layer, 0, WA_BRANCH_COLS, tm, tn)
    proj_d = _inproj_call(xn, p["w_t"], layer, WD_START, WD_BRANCH_COLS, tm, tn)
    return xn, dtraw, proj_a, proj_d


def _dense_tail(x, xn, outs, layer, p, tm_merge, tn_merge, tm_out):
    merged = _merge_call(xn, outs, layer, p, tm_merge, tn_merge)
    return _out_call(merged, x, layer, p, tm_out)


def _prompt_layer(x, n_seq, seq, layer, p):
    xn, dtraw, proj_a, proj_d = _dense_head(x, layer, p, 256, 1024, 1024)
    out_a, conv_a = _conva_prompt_call(proj_a, n_seq, seq, layer, p)
    y_b, s_re, s_im = _s5_prompt_call(proj_a, n_seq, seq, layer, p)
    out_b = _s5_glu_call(y_b, proj_a, layer, p, 512)
    out_c, ssd_st, conv_ssd = _ssd_prompt_call(proj_a, dtraw, n_seq, seq, layer, p)
    out_d = _mlp_prompt_call(proj_d, n_seq, seq, layer, p)
    y = _dense_tail(x, xn, (out_a, out_b, out_c, out_d), layer, p, 512, 256, 256)
    return y, conv_a, s_re, s_im, ssd_st, conv_ssd


def _sample_layer(x, cache_a_t, h_re, h_im, h_ssd, buf_ssd_t, layer, p, ssd_stack):
    ns = x.shape[0]
    xn, dtraw, proj_a, proj_d = _dense_head(x, layer, p, ns, ns, 1024)
    out_a, conv_a_t = _conva_sample_call(proj_a, cache_a_t, layer, p)
    y_b, s_re, s_im = _s5_sample_call(proj_a, h_re, h_im, layer, p)
    out_b = _s5_glu_call(y_b, proj_a, layer, p, ns)
    out_c, ssd_st, conv_ssd_t = _ssd_sample_call(proj_a, dtraw, buf_ssd_t, h_ssd, layer, p, ssd_stack)
    out_d, v_d = _mlp_sample_call(proj_d, layer, p)
    y = _dense_tail(x, xn, (out_a, out_b, out_c, out_d), layer, p, ns, 256, ns)
    return y, conv_a_t, s_re, s_im, ssd_st, conv_ssd_t, v_d


def kernel(x_prompt, x_sample, cache_conv_a, state_s5_re, state_s5_im, state_ssd, cache_conv_ssd, norm_pre_g, w_in, conv_a_w, conv_a_b, ln_a_g, ln_a_b, s5_a_re, s5_a_im, s5_log_step, s5_b_re, s5_b_im, s5_c_re, s5_c_im, s5_d, s5_w_glu, s5_b_glu, ssd_conv_w, ssd_conv_b, ssd_dt_bias, ssd_a_log, ssd_d, ssd_norm_g, mlp_ln_g, mlp_ln_b, mlp_w_s, mlp_b_s, w_branch, w_out, norm_post_g):
    w = dict(norm_pre_g=norm_pre_g, w_in=w_in, conv_a_w=conv_a_w, conv_a_b=conv_a_b, ln_a_g=ln_a_g,
             ln_a_b=ln_a_b, s5_a_re=s5_a_re, s5_a_im=s5_a_im, s5_log_step=s5_log_step, s5_b_re=s5_b_re,
             s5_b_im=s5_b_im, s5_c_re=s5_c_re, s5_c_im=s5_c_im, s5_d=s5_d, s5_w_glu=s5_w_glu,
             s5_b_glu=s5_b_glu, ssd_conv_w=ssd_conv_w, ssd_conv_b=ssd_conv_b, ssd_dt_bias=ssd_dt_bias,
             ssd_a_log=ssd_a_log, ssd_d=ssd_d, ssd_norm_g=ssd_norm_g, mlp_ln_g=mlp_ln_g, mlp_ln_b=mlp_ln_b,
             mlp_w_s=mlp_w_s, mlp_b_s=mlp_b_s, w_branch=w_branch, w_out=w_out, norm_post_g=norm_post_g)
    depth = w_in.shape[0]
    n_seq, seq, _ = x_prompt.shape
    ns = x_sample.shape[0]
    p = _prep_params(w)
    hp = x_prompt.reshape(n_seq * seq, D_MODEL)
    hs = x_sample.reshape(ns, D_MODEL)
    cache_a_t = jnp.swapaxes(cache_conv_a, 1, 2)
    buf_ssd_t = jnp.swapaxes(cache_conv_ssd, 1, 2)
    h_re = state_s5_re.reshape(depth, ns, S5_LANES)
    h_im = state_s5_im.reshape(depth, ns, S5_LANES)
    h_ssd = state_ssd.reshape(depth, ns, SSD_GROUPS, SSD_GW, SSD_STATE)
    acc = [[] for _ in range(11)]
    ssd_stack = None
    for i in range(depth):
        hp, a1, r1, m1, s1, c1 = _prompt_layer(hp, n_seq, seq, i, p)
        hs, a2, r2, m2, ssd_stack, c2, v2 = _sample_layer(hs, cache_a_t, h_re, h_im, h_ssd, buf_ssd_t, i, p,
                                                          ssd_stack)
        vals = (a1, a2,
                r1.reshape(n_seq, S5_GROUPS, S5_STATE), r2.reshape(ns, S5_GROUPS, S5_STATE),
                m1.reshape(n_seq, S5_GROUPS, S5_STATE), m2.reshape(ns, S5_GROUPS, S5_STATE),
                s1.reshape(n_seq, SSD_GROUPS, 4, 64, SSD_STATE), None,
                c1, c2, v2.reshape(ns, 1, W_BRANCH))
        for lst, v in zip(acc, vals):
            lst.append(v)
    st = [None if l[0] is None else jnp.stack(l) for l in acc]
    st[1] = jnp.swapaxes(st[1], 1, 2)
    st[7] = ssd_stack.reshape(depth, ns, SSD_GROUPS, 4, 64, SSD_STATE)
    st[9] = jnp.swapaxes(st[9], 1, 2)
    return (hp.reshape(n_seq, seq, D_MODEL), hs.reshape(ns, 1, D_MODEL)) + tuple(st)
```

```python
import jax
import jax.numpy as jnp
from jax import lax
from jax.experimental import pallas as pl
from jax.experimental.pallas import tpu as pltpu

F32 = jnp.float32
BF16 = jnp.bfloat16
EPS = 1e-6

D_MODEL = 4096
W_BRANCH = 2048
N_BRANCH = 4
LANE = 128
CONV_A_WIDTH = 31
S5_GROUPS = 128
S5_GROUP = 16
S5_STATE = 64
S5_LANES = S5_GROUPS * S5_STATE
S5_GB = 8
S5_NB = S5_GROUPS // S5_GB
S5_BW = S5_GB * S5_STATE
S5_UW = S5_GB * S5_GROUP
S5_FOLD = 4
SSD_HEADS = 32
SSD_GROUPS = 8
SSD_STATE = 128
SSD_CHUNK = 128
SSD_CONV_DIM = 4096
SSD_GW = W_BRANCH // SSD_GROUPS
SSD_B_OFF = W_BRANCH
SSD_C_OFF = W_BRANCH + SSD_GROUPS * SSD_STATE
MLP_CHUNK = 128
MLP_GROUPS = 16
MLP_GD = W_BRANCH // MLP_GROUPS
DT_PAD = LANE
WA_BRANCH_COLS = 16384
WD_START = WA_BRANCH_COLS + SSD_HEADS
WD_BRANCH_COLS = 3 * W_BRANCH
WM_START = WD_START + WD_BRANCH_COLS
COL_A_VAL, COL_A_G, COL_A_GATE, COL_B_U, COL_B_GATE, COL_C_Z = 0, 1, 2, 3, 4, 5
COL_XBC4096 = 3
COL_D_U, COL_D_V, COL_D_GATE = 0, 1, 2

VMEM_LIMIT = 56 * 1024 * 1024
VMEM_LIMIT_RESIDENT = 60 * 1024 * 1024


def _cp(*sem, vmem=VMEM_LIMIT):
    return pltpu.CompilerParams(dimension_semantics=sem, vmem_limit_bytes=vmem)


def _vec_spec(width, layer):
    return pl.BlockSpec((None, 1, width), lambda *_: (layer, 0, 0))


def _layer_spec(shape, layer):
    return pl.BlockSpec((None,) + tuple(shape), lambda *_: (layer,) + (0,) * len(shape))


def _sigmoid(x):
    return jax.nn.sigmoid(x)


def _silu(x):
    return x * jax.nn.sigmoid(x)


def _gelu(x):
    return jax.nn.gelu(x)


def _dot(a, b):
    return jnp.dot(a, b, preferred_element_type=F32)


def _dot_nt(a, b):
    return lax.dot_general(a, b, (((1,), (1,)), ((), ())), preferred_element_type=F32)


def _split3(x):
    hi = x.astype(BF16)
    r1 = x - hi.astype(F32)
    mid = r1.astype(BF16)
    lo = (r1 - mid.astype(F32)).astype(BF16)
    return hi, mid, lo


def _dot_exact_rhs(a_bf16, x):
    hi, mid, lo = _split3(x)
    return _dot(a_bf16, hi) + _dot(a_bf16, mid) + _dot(a_bf16, lo)


def _dot_exact_lhs(x, e_bf16):
    hi, mid, lo = _split3(x)
    return _dot(hi, e_bf16) + _dot(mid, e_bf16) + _dot(lo, e_bf16)


def _norm_kernel(x_ref, g_ref, wdt_ref, xn_ref, dt_ref):
    x = x_ref[...]
    y = x * lax.rsqrt(jnp.mean(x * x, axis=-1, keepdims=True) + EPS) * g_ref[...]
    xb = y.astype(BF16)
    xn_ref[...] = xb
    d = _dot_nt(xb, wdt_ref[...].astype(BF16))
    lane = lax.broadcasted_iota(jnp.int32, d.shape, 1)
    dt_ref[...] = jnp.where(lane < SSD_HEADS, d, 0.0)


def _norm_call(x, g, w_t, layer, tm):
    m = x.shape[0]
    return pl.pallas_call(
        _norm_kernel,
        grid=(m // tm,),
        in_specs=[pl.BlockSpec((tm, D_MODEL), lambda i: (i, 0)),
                  _vec_spec(D_MODEL, layer),
                  pl.BlockSpec((None, DT_PAD, D_MODEL), lambda i: (layer, WA_BRANCH_COLS // DT_PAD, 0))],
        out_specs=[pl.BlockSpec((tm, D_MODEL), lambda i: (i, 0)),
                   pl.BlockSpec((tm, DT_PAD), lambda i: (i, 0))],
        out_shape=[jax.ShapeDtypeStruct((m, D_MODEL), BF16),
                   jax.ShapeDtypeStruct((m, DT_PAD), F32)],
        compiler_params=_cp("arbitrary"),
        name="norm_dt",
    )(x, g, w_t)


W_ROW_ALIGN = 32


def _wt_spec(layer, row0, tn):
    return pl.BlockSpec((pl.Element(1), pl.Element(tn), pl.Element(D_MODEL)),
                        lambda i, j: (layer, pl.multiple_of(row0 + j * tn, W_ROW_ALIGN), 0))


def _mm_kernel(x_ref, w_ref, o_ref):
    o_ref[...] = _dot_nt(x_ref[...], w_ref[0].astype(BF16)).astype(o_ref.dtype)


def _inproj_call(xn, w_t, layer, row0, n_cols, tm, tn):
    m, k = xn.shape
    return pl.pallas_call(
        _mm_kernel,
        grid=(m // tm, n_cols // tn),
        in_specs=[pl.BlockSpec((tm, k), lambda i, j: (i, 0)),
                  _wt_spec(layer, row0, tn)],
        out_specs=pl.BlockSpec((tm, tn), lambda i, j: (i, j)),
        out_shape=jax.ShapeDtypeStruct((m, n_cols), F32),
        compiler_params=_cp("arbitrary", "arbitrary"),
        name="in_proj",
    )(xn, w_t)


CA_TC = 256
CA_HALO = 32
CA_RB = 64
W_TILES = W_BRANCH // LANE


def _ln_swish_gate(v, lg, lb, gate):
    mu = jnp.mean(v, axis=-1, keepdims=True)
    xc = v - mu
    y = xc * lax.rsqrt(jnp.mean(xc * xc, axis=-1, keepdims=True) + EPS) * lg + lb
    return _silu(y) * _silu(gate)


def _conva_prompt_kernel(av_ref, ag_ref, gate_ref, w_ref, b_ref, lg_ref, lb_ref,
                         o_ref, nc_ref, ext_ref, cv_ref):
    c = pl.program_id(1)
    tc = CA_TC
    off = CA_HALO - (CONV_A_WIDTH - 1)

    @pl.when(c == 0)
    def _():
        ext_ref[:, 0:CA_HALO, :] = jnp.zeros((W_TILES, CA_HALO, LANE), F32)

    for lt in range(W_TILES):
        ls = slice(lt * LANE, (lt + 1) * LANE)
        ext_ref[lt, CA_HALO:CA_HALO + tc, :] = av_ref[:, ls] * _sigmoid(ag_ref[:, ls])

    def conv_tile(lt, carry):
        ls = pl.ds(pl.multiple_of(lt * LANE, LANE), LANE)
        for r0 in range(0, tc, CA_RB):
            accs = [jnp.broadcast_to(b_ref[:, ls], (8, LANE))] * (CA_RB // 8)
            for k in range(CONV_A_WIDTH):
                wk = jnp.broadcast_to(w_ref[k:k + 1, ls], (8, LANE))
                accs = [a + wk * ext_ref[lt, r0 + off + k + 8 * i:r0 + off + k + 8 * i + 8, :]
                        for i, a in enumerate(accs)]
            for i, a in enumerate(accs):
                cv_ref[r0 + 8 * i:r0 + 8 * i + 8, ls] = a
        return carry

    lax.fori_loop(0, W_TILES, conv_tile, 0)

    o_ref[...] = _ln_swish_gate(cv_ref[...], lg_ref[...], lb_ref[...], gate_ref[...]).astype(o_ref.dtype)

    @pl.when(c == pl.num_programs(1) - 1)
    def _():
        for lt in range(W_TILES):
            nc_ref[:, lt * LANE:(lt + 1) * LANE] = ext_ref[lt, tc + off:tc + CA_HALO, :]

    for lt in range(W_TILES):
        ext_ref[lt, 0:CA_HALO, :] = ext_ref[lt, tc:tc + CA_HALO, :]


def _conva_prompt_call(proj, n_seq, seq, layer, p):
    nck = seq // CA_TC
    row = lambda n, c: n * nck + c
    vec = _vec_spec(W_BRANCH, layer)
    return pl.pallas_call(
        _conva_prompt_kernel,
        grid=(n_seq, nck),
        in_specs=[pl.BlockSpec((CA_TC, W_BRANCH), lambda n, c: (row(n, c), COL_A_VAL)),
                  pl.BlockSpec((CA_TC, W_BRANCH), lambda n, c: (row(n, c), COL_A_G)),
                  pl.BlockSpec((CA_TC, W_BRANCH), lambda n, c: (row(n, c), COL_A_GATE)),
                  _layer_spec((CONV_A_WIDTH, W_BRANCH), layer),
                  vec, vec, vec],
        out_specs=[pl.BlockSpec((CA_TC, W_BRANCH), lambda n, c: (row(n, c), 0)),
                   pl.BlockSpec((None, CONV_A_WIDTH - 1, W_BRANCH), lambda n, c: (n, 0, 0))],
        out_shape=[jax.ShapeDtypeStruct((n_seq * seq, W_BRANCH), BF16),
                   jax.ShapeDtypeStruct((n_seq, CONV_A_WIDTH - 1, W_BRANCH), F32)],
        scratch_shapes=[pltpu.VMEM((W_TILES, CA_HALO + CA_TC, LANE), F32),
                        pltpu.VMEM((CA_TC, W_BRANCH), F32)],
        compiler_params=_cp("arbitrary", "arbitrary"),
        name="conv_a_prompt",
    )(proj, proj, proj, p["conv_a_w"], p["conv_a_b"], p["ln_a_g"], p["ln_a_b"])


SAMPLE_NB = 8


def _conva_sample_kernel(av_ref, ag_ref, gate_ref, cache_ref, w_ref, b_ref, lg_ref, lb_ref,
                         o_ref, nc_ref):
    h = av_ref[...] * _sigmoid(ag_ref[...])
    kw = CONV_A_WIDTH - 1
    acc = b_ref[...] + w_ref[kw:kw + 1, :] * h
    for k in range(kw):
        acc = acc + w_ref[k:k + 1, :] * cache_ref[k]
    o_ref[...] = _ln_swish_gate(acc, lg_ref[...], lb_ref[...], gate_ref[...]).astype(o_ref.dtype)
    for k in range(kw - 1):
        nc_ref[k] = cache_ref[k + 1]
    nc_ref[kw - 1] = h


def _conva_sample_call(proj, cache_t, layer, p):
    ns = proj.shape[0]
    nb = SAMPLE_NB
    kw = CONV_A_WIDTH - 1
    vec = _vec_spec(W_BRANCH, layer)
    return pl.pallas_call(
        _conva_sample_kernel,
        grid=(ns // nb,),
        in_specs=[pl.BlockSpec((nb, W_BRANCH), lambda i: (i, COL_A_VAL)),
                  pl.BlockSpec((nb, W_BRANCH), lambda i: (i, COL_A_G)),
                  pl.BlockSpec((nb, W_BRANCH), lambda i: (i, COL_A_GATE)),
                  pl.BlockSpec((None, kw, nb, W_BRANCH), lambda i: (layer, 0, i, 0)),
                  _layer_spec((CONV_A_WIDTH, W_BRANCH), layer),
                  vec, vec, vec],
        out_specs=[pl.BlockSpec((nb, W_BRANCH), lambda i: (i, 0)),
                   pl.BlockSpec((kw, nb, W_BRANCH), lambda i: (0, i, 0))],
        out_shape=[jax.ShapeDtypeStruct((ns, W_BRANCH), BF16),
                   jax.ShapeDtypeStruct((kw, ns, W_BRANCH), F32)],
        compiler_params=_cp("arbitrary"),
        name="conv_a_sample",
    )(proj, proj, proj, cache_t, p["conv_a_w"], p["conv_a_b"], p["ln_a_g"], p["ln_a_b"])


def _cmul(ar, ai, br, bi):
    return ar * br - ai * bi, ar * bi + ai * br


def _s5_prep_kernel(are_ref, aim_ref, ls_ref, bre_ref, bim_ref, abre_ref, abim_ref, *bb_refs):
    lam_re = are_ref[...]
    lam_im = aim_ref[...]
    step = jnp.exp(ls_ref[...])
    mag = jnp.exp(lam_re * step)
    ab_re = mag * jnp.cos(lam_im * step)
    ab_im = mag * jnp.sin(lam_im * step)
    den = lam_re * lam_re + lam_im * lam_im
    nr = ab_re - 1.0
    coef_re = (nr * lam_re + ab_im * lam_im) / den
    coef_im = (ab_im * lam_re - nr * lam_im) / den
    abre_ref[...] = ab_re
    abim_ref[...] = ab_im
    cur_re, cur_im = _cmul(coef_re, coef_im, bre_ref[...], bim_ref[...])
    for k in range(S5_FOLD):
        bb_refs[2 * k][...] = cur_re
        bb_refs[2 * k + 1][...] = cur_im
        cur_re, cur_im = _cmul(ab_re, ab_im, cur_re, cur_im)


def _s5_prep_call(a_re, a_im, log_step, b_re, b_im):
    rows = a_re.shape[0]
    rb = 1024
    col = pl.BlockSpec((rb, 1), lambda i: (i, 0))
    mat = pl.BlockSpec((rb, S5_GROUP), lambda i: (i, 0))
    col_sh = jax.ShapeDtypeStruct((rows, 1), F32)
    mat_sh = jax.ShapeDtypeStruct((rows, S5_GROUP), F32)
    return pl.pallas_call(
        _s5_prep_kernel,
        grid=(rows // rb,),
        in_specs=[col, col, col, mat, mat],
        out_specs=[col, col] + [mat] * (2 * S5_FOLD),
        out_shape=[col_sh, col_sh] + [mat_sh] * (2 * S5_FOLD),
        compiler_params=_cp("arbitrary"),
        name="s5_prep",
    )(a_re, a_im, log_step, b_re, b_im)


S5_TC = 2048
S5_UHEAD = 8
S5_SUB = 2048


def _s5_prompt_kernel(u_ref, abr_ref, abi_ref, wbr_ref, wbi_ref, wcr_ref, wci_ref, d_ref,
                      y_ref, sre_ref, sim_ref,
                      ush, hre, him, l4r, l4i, pwr, pwi, car, cai):
    c = pl.program_id(2)
    tc = S5_TC
    bw = S5_BW

    @pl.when(c == 0)
    def _():
        row = lax.broadcasted_iota(jnp.int32, (8, bw), 0)
        a1r = jnp.broadcast_to(abr_ref[...], (8, bw))
        a1i = jnp.broadcast_to(abi_ref[...], (8, bw))
        cr, ci = a1r, a1i
        pr = jnp.zeros((8, bw), F32)
        pi = jnp.zeros((8, bw), F32)
        for r in range(8):
            pr = jnp.where(row == r, cr, pr)
            pi = jnp.where(row == r, ci, pi)
            if r + 1 == S5_FOLD:
                l4r[...] = jnp.where(row >= S5_FOLD, cr, 0.0)
                l4i[...] = jnp.where(row >= S5_FOLD, ci, 0.0)
            cr, ci = _cmul(cr, ci, a1r, a1i)
        pwr[...] = pr
        pwi[...] = pi
        car[...] = jnp.zeros(car.shape, F32)
        cai[...] = jnp.zeros(cai.shape, F32)
        ush[0:S5_UHEAD, :] = jnp.zeros((S5_UHEAD, S5_UW), F32)

    ush[S5_UHEAD:S5_UHEAD + tc, :] = u_ref[...]
    sub = S5_SUB
    row8 = lax.broadcasted_iota(jnp.int32, (sub, S5_UW), 0) & 7
    cr = car[...]
    ci = cai[...]
    for r0 in range(0, tc, sub):
        u = u_ref[r0:r0 + sub, :]
        parts = [u.astype(BF16)]
        for k in range(1, S5_FOLD):
            uk = ush[S5_UHEAD - k + r0:S5_UHEAD - k + r0 + sub, :]
            parts.append(jnp.where(row8 >= k, uk, 0.0).astype(BF16))
        lhs = jnp.concatenate(parts, axis=-1)
        hre[r0:r0 + sub, :] = _dot_nt(lhs, wbr_ref[...])
        him[r0:r0 + sub, :] = _dot_nt(lhs, wbi_ref[...])
        for t0 in range(r0, r0 + sub, 8):
            xr = hre[t0:t0 + 8, :]
            xi = him[t0:t0 + 8, :]
            tr, ti = _cmul(l4r[...], l4i[...], pltpu.roll(xr, S5_FOLD, 0), pltpu.roll(xi, S5_FOLD, 0))
            xr = xr + tr
            xi = xi + ti
            tr, ti = _cmul(pwr[...], pwi[...], cr, ci)
            xr = xr + tr
            xi = xi + ti
            hre[t0:t0 + 8, :] = xr
            him[t0:t0 + 8, :] = xi
            cr = jnp.broadcast_to(xr[7:8, :], (8, bw))
            ci = jnp.broadcast_to(xi[7:8, :], (8, bw))
        y_ref[r0:r0 + sub, :] = (_dot_nt(hre[r0:r0 + sub, :].astype(BF16), wcr_ref[...])
                                 - _dot_nt(him[r0:r0 + sub, :].astype(BF16), wci_ref[...])
                                 + d_ref[...] * u)
    car[...] = cr
    cai[...] = ci

    @pl.when(c == pl.num_programs(2) - 1)
    def _():
        sre_ref[...] = cr[0:1, :]
        sim_ref[...] = ci[0:1, :]


def _s5_prompt_call(proj, n_seq, seq, layer, p):
    nck = seq // S5_TC
    ub = W_BRANCH // S5_UW
    lane_blk = lambda w: pl.BlockSpec((None, 1, w), lambda n, j, c: (layer, 0, j))
    wspec = lambda r, cc: pl.BlockSpec((None, None, r, cc), lambda n, j, c: (layer, j, 0, 0))
    sspec = pl.BlockSpec((None, 1, S5_BW), lambda n, j, c: (n, 0, j))
    return pl.pallas_call(
        _s5_prompt_kernel,
        grid=(n_seq, S5_NB, nck),
        in_specs=[pl.BlockSpec((S5_TC, S5_UW), lambda n, j, c: (n * nck + c, COL_B_U * ub + j)),
                  lane_blk(S5_BW), lane_blk(S5_BW),
                  wspec(S5_BW, S5_FOLD * S5_UW), wspec(S5_BW, S5_FOLD * S5_UW),
                  wspec(S5_UW, S5_BW), wspec(S5_UW, S5_BW),
                  lane_blk(S5_UW)],
        out_specs=[pl.BlockSpec((S5_TC, S5_UW), lambda n, j, c: (n * nck + c, j)), sspec, sspec],
        out_shape=[jax.ShapeDtypeStruct((n_seq * seq, W_BRANCH), F32),
                   jax.ShapeDtypeStruct((n_seq, 1, S5_LANES), F32),
                   jax.ShapeDtypeStruct((n_seq, 1, S5_LANES), F32)],
        scratch_shapes=[pltpu.VMEM((S5_UHEAD + S5_TC, S5_UW), F32),
                        pltpu.VMEM((S5_TC, S5_BW), F32), pltpu.VMEM((S5_TC, S5_BW), F32)]
        + [pltpu.VMEM((8, S5_BW), F32)] * 6,
        compiler_params=_cp("arbitrary", "arbitrary", "arbitrary"),
        name="s5_prompt",
    )(proj, p["ab_re"], p["ab_im"], p["wb_re"], p["wb_im"], p["wc_re"], p["wc_im"], p["s5_d"])


def _s5_sample_kernel(u_ref, h0r_ref, h0i_ref, abr_ref, abi_ref, wbr_ref, wbi_ref, wcr_ref, wci_ref,
                      d_ref, y_ref, sre_ref, sim_ref):
    for j in range(S5_NB):
        us = slice(j * S5_UW, (j + 1) * S5_UW)
        ss = slice(j * S5_BW, (j + 1) * S5_BW)
        ub = u_ref[:, us].astype(BF16)
        ar = abr_ref[:, ss]
        ai = abi_ref[:, ss]
        h0r = h0r_ref[:, ss]
        h0i = h0i_ref[:, ss]
        nr = ar * h0r - ai * h0i + _dot_nt(ub, wbr_ref[j])
        ni = ar * h0i + ai * h0r + _dot_nt(ub, wbi_ref[j])
        sre_ref[:, ss] = nr
        sim_ref[:, ss] = ni
        y_ref[:, us] = (_dot_nt(nr.astype(BF16), wcr_ref[j]) - _dot_nt(ni.astype(BF16), wci_ref[j])
                        + d_ref[:, us] * u_ref[:, us])


def _s5_sample_call(proj, h0_re, h0_im, layer, p):
    ns = proj.shape[0]
    tb = 64
    h0spec = pl.BlockSpec((None, tb, S5_LANES), lambda i: (layer, i, 0))
    sspec = pl.BlockSpec((tb, S5_LANES), lambda i: (i, 0))
    wb = pl.BlockSpec((None, S5_NB, S5_BW, S5_UW), lambda i: (layer, 0, 0, 0))
    wc = pl.BlockSpec((None, S5_NB, S5_UW, S5_BW), lambda i: (layer, 0, 0, 0))
    return pl.pallas_call(
        _s5_sample_kernel,
        grid=(ns // tb,),
        in_specs=[pl.BlockSpec((tb, W_BRANCH), lambda i: (i, COL_B_U)), h0spec, h0spec,
                  _vec_spec(S5_LANES, layer), _vec_spec(S5_LANES, layer), wb, wb, wc, wc,
                  _vec_spec(W_BRANCH, layer)],
        out_specs=[pl.BlockSpec((tb, W_BRANCH), lambda i: (i, 0)), sspec, sspec],
        out_shape=[jax.ShapeDtypeStruct((ns, W_BRANCH), F32),
                   jax.ShapeDtypeStruct((ns, S5_LANES), F32),
                   jax.ShapeDtypeStruct((ns, S5_LANES), F32)],
        compiler_params=_cp("arbitrary"),
        name="s5_sample",
    )(proj, h0_re, h0_im, p["ab_re"], p["ab_im"], p["wb_re"], p["wb_im"], p["wc_re"], p["wc_im"], p["s5_d"])


def _s5_glu_kernel(y_ref, gate_ref, w_ref, b_ref, o_ref):
    g = _gelu(y_ref[...])
    z = _dot(g.astype(BF16), w_ref[...]) + b_ref[...]
    o_ref[...] = (g * _sigmoid(z) * _silu(gate_ref[...])).astype(o_ref.dtype)


def _s5_glu_call(y, proj, layer, p, tm):
    m = y.shape[0]
    return pl.pallas_call(
        _s5_glu_kernel,
        grid=(m // tm,),
        in_specs=[pl.BlockSpec((tm, W_BRANCH), lambda i: (i, 0)),
                  pl.BlockSpec((tm, W_BRANCH), lambda i: (i, COL_B_GATE)),
                  _layer_spec((W_BRANCH, W_BRANCH), layer),
                  _vec_spec(W_BRANCH, layer)],
        out_specs=pl.BlockSpec((tm, W_BRANCH), lambda i: (i, 0)),
        out_shape=jax.ShapeDtypeStruct((m, W_BRANCH), BF16),
        compiler_params=_cp("arbitrary"),
        name="s5_glu",
    )(y, proj, p["s5_w_glu"], p["s5_b_glu"])


SSD_EXT_HEAD = 8
SSD_TAPS = 4
XBC_TILES = SSD_CONV_DIM // LANE


def _softplus(x):
    return jnp.maximum(x, 0.0) + jnp.log(1.0 + jnp.exp(-jnp.abs(x)))


def _ssd_prompt_kernel(z_ref, xbc_ref, dtr_ref, cw_ref, cb_ref, dtb_ref, alog_ref, dvec_ref, ng_ref,
                       o_ref, st_ref, nb_ref,
                       ext, h, xa, ybuf):
    c = pl.program_id(1)
    q = SSD_CHUNK
    e0 = SSD_EXT_HEAD
    off = e0 - (SSD_TAPS - 1)

    @pl.when(c == 0)
    def _():
        ext[:, 0:e0, :] = jnp.zeros((XBC_TILES, e0, LANE), F32)
        h[...] = jnp.zeros(h.shape, F32)

    rb = 64

    def conv_tile(lt, carry):
        ls = pl.ds(pl.multiple_of(lt * LANE, LANE), LANE)
        ext[lt, e0:e0 + q, :] = xbc_ref[:, ls]
        for r0 in range(0, q, rb):
            accs = [jnp.broadcast_to(cb_ref[:, ls], (8, LANE))] * (rb // 8)
            for k in range(SSD_TAPS):
                wk = jnp.broadcast_to(cw_ref[k:k + 1, ls], (8, LANE))
                accs = [a + wk * ext[lt, r0 + off + k + 8 * i:r0 + off + k + 8 * i + 8, :]
                        for i, a in enumerate(accs)]
            for i, a in enumerate(accs):
                xa[r0 + 8 * i:r0 + 8 * i + 8, ls] = _silu(a)
        return carry

    lax.fori_loop(0, XBC_TILES, conv_tile, 0)

    @pl.when(c == pl.num_programs(1) - 1)
    def _():
        for lt in range(XBC_TILES):
            nb_ref[:, lt * LANE:(lt + 1) * LANE] = ext[lt, q + off:q + e0, :]

    for lt in range(XBC_TILES):
        ext[lt, 0:e0, :] = ext[lt, q:q + e0, :]

    dt = _softplus(dtr_ref[...] + dtb_ref[...])
    a = -jnp.exp(alog_ref[...])
    da = dt * a
    row = lax.broadcasted_iota(jnp.int32, (q, q), 0)
    col = lax.broadcasted_iota(jnp.int32, (q, q), 1)
    tri = row >= col
    lt_ones = jnp.where(tri, 1.0, 0.0).astype(BF16)
    dacs = _dot_exact_rhs(lt_ones, da)
    dcs_t = dacs.T
    dt_t = dt.T
    tot_t = jnp.broadcast_to(dcs_t[:, q - 1:q], (q, q))
    w_t = dt_t * jnp.exp(tot_t - dcs_t)
    cd_t = jnp.exp(tot_t)
    lane_lo = col < 64
    row_lo = row < 64

    def rows_of(m, r):
        return jnp.broadcast_to(m[r:r + 1, :], (q, q))

    for g in range(SSD_GROUPS):
        bm_b = xa[:, SSD_B_OFF + g * SSD_STATE:SSD_B_OFF + (g + 1) * SSD_STATE].astype(BF16)
        cm_b = xa[:, SSD_C_OFF + g * SSD_STATE:SSD_C_OFF + (g + 1) * SSD_STATE].astype(BF16)
        cb = _dot_nt(cm_b, bm_b)
        for p in range(2):
            r_lo = 4 * g + 2 * p
            cs = slice(g * SSD_GW + p * LANE, g * SSD_GW + (p + 1) * LANE)
            xs = xa[:, cs]
            ydiag = None
            eo = []
            for hh in range(2):
                r = r_lo + hh
                a_col = jnp.broadcast_to(dacs[:, r:r + 1], (q, q))
                seg = a_col - rows_of(dcs_t, r)
                lmat = jnp.where(tri, jnp.exp(seg), 0.0)
                mp = cb * lmat * rows_of(dt_t, r)
                keep = lane_lo if hh == 0 else jnp.logical_not(lane_lo)
                xs_h = jnp.where(keep, xs, 0.0)
                term = _dot(mp.astype(BF16), xs_h.astype(BF16))
                ydiag = term if ydiag is None else ydiag + term
                eo.append(jnp.exp(a_col))
            eo_pair = jnp.where(lane_lo, eo[0], eo[1])
            hs = slice(p * LANE, (p + 1) * LANE)
            h_pair = h[g, hs, :]
            yoff = _dot_nt(cm_b, h_pair.astype(BF16)) * eo_pair
            ybuf[:, cs] = ydiag + yoff + dvec_ref[:, cs] * xs
            w_rows = jnp.where(row_lo, rows_of(w_t, r_lo), rows_of(w_t, r_lo + 1))
            cd_rows = jnp.where(row_lo, rows_of(cd_t, r_lo), rows_of(cd_t, r_lo + 1))
            st = _dot((xs.T * w_rows).astype(BF16), bm_b)
            h[g, hs, :] = h_pair * cd_rows + st

    y = ybuf[...] * _silu(z_ref[...])
    for g in range(SSD_GROUPS):
        gs = slice(g * SSD_GW, (g + 1) * SSD_GW)
        yg = y[:, gs]
        o_ref[:, gs] = (yg * lax.rsqrt(jnp.mean(yg * yg, axis=-1, keepdims=True) + EPS)
                        * ng_ref[:, gs]).astype(o_ref.dtype)

    @pl.when(c == pl.num_programs(1) - 1)
    def _():
        st_ref[...] = h[...]


def _ssd_prompt_call(proj, dtraw, n_seq, seq, layer, p):
    q = SSD_CHUNK
    nck = seq // q
    row = lambda n, c: n * nck + c
    return pl.pallas_call(
        _ssd_prompt_kernel,
        grid=(n_seq, nck),
        in_specs=[pl.BlockSpec((q, W_BRANCH), lambda n, c: (row(n, c), COL_C_Z)),
                  pl.BlockSpec((q, SSD_CONV_DIM), lambda n, c: (row(n, c), COL_XBC4096)),
                  pl.BlockSpec((q, DT_PAD), lambda n, c: (row(n, c), 0)),
                  _layer_spec((SSD_TAPS, SSD_CONV_DIM), layer),
                  _vec_spec(SSD_CONV_DIM, layer),
                  _vec_spec(DT_PAD, layer), _vec_spec(DT_PAD, layer),
                  _vec_spec(W_BRANCH, layer), _vec_spec(W_BRANCH, layer)],
        out_specs=[pl.BlockSpec((q, W_BRANCH), lambda n, c: (row(n, c), 0)),
                   pl.BlockSpec((None, SSD_GROUPS, SSD_GW, SSD_STATE), lambda n, c: (n, 0, 0, 0)),
                   pl.BlockSpec((None, SSD_TAPS - 1, SSD_CONV_DIM), lambda n, c: (n, 0, 0))],
        out_shape=[jax.ShapeDtypeStruct((n_seq * seq, W_BRANCH), BF16),
                   jax.ShapeDtypeStruct((n_seq, SSD_GROUPS, SSD_GW, SSD_STATE), F32),
                   jax.ShapeDtypeStruct((n_seq, SSD_TAPS - 1, SSD_CONV_DIM), F32)],
        scratch_shapes=[pltpu.VMEM((XBC_TILES, SSD_EXT_HEAD + q, LANE), F32),
                        pltpu.VMEM((SSD_GROUPS, SSD_GW, SSD_STATE), F32),
                        pltpu.VMEM((q, SSD_CONV_DIM), F32),
                        pltpu.VMEM((q, W_BRANCH), F32)],
        compiler_params=_cp("arbitrary", "arbitrary"),
        name="ssd_prompt",
    )(proj, proj, dtraw, p["ssd_conv_w"], p["ssd_conv_b"], p["ssd_dt_bias"], p["ssd_a_log"],
      p["ssd_d"], p["ssd_norm_g"])


def _ssd_sample_kernel(z_ref, xbc_ref, dtr_ref, buf_ref, h0_ref, cw_ref, cb_ref, dtb_ref, alog_ref,
                       dvec_ref, ng_ref, ex_ref,
                       o_ref, st_ref, nb_ref, colx, cold):
    nb = SAMPLE_NB
    xbc = xbc_ref[...]
    acc = cb_ref[...] + cw_ref[SSD_TAPS - 1:SSD_TAPS, :] * xbc
    for k in range(SSD_TAPS - 1):
        acc = acc + cw_ref[k:k + 1, :] * buf_ref[k]
    xa = _silu(acc)
    for k in range(SSD_TAPS - 2):
        nb_ref[k] = buf_ref[k + 1]
    nb_ref[SSD_TAPS - 2] = xbc

    dt = _softplus(dtr_ref[...] + dtb_ref[...])
    a = -jnp.exp(alog_ref[...])
    dec = jnp.exp(dt * a)
    ex = ex_ref[...]
    dt_x = _dot_exact_lhs(dt, ex)
    dec_x = _dot_exact_lhs(dec, ex)
    xs = xa[:, 0:W_BRANCH]
    xdt = xs * dt_x

    pad = jnp.zeros((LANE - nb, W_BRANCH), F32)
    colx[...] = jnp.concatenate([xdt, pad], axis=0).T
    cold[...] = jnp.concatenate([dec_x, pad], axis=0).T

    rown = lax.broadcasted_iota(jnp.int32, (nb, SSD_GW), 0)
    for g in range(SSD_GROUPS):
        gs = slice(g * SSD_GW, (g + 1) * SSD_GW)
        bm = xa[:, SSD_B_OFF + g * SSD_STATE:SSD_B_OFF + (g + 1) * SSD_STATE]
        cm = xa[:, SSD_C_OFF + g * SSD_STATE:SSD_C_OFF + (g + 1) * SSD_STATE]
        cm_b = cm.astype(BF16)
        cbv = jnp.sum(cm * bm, axis=-1, keepdims=True)
        yoff = jnp.zeros((nb, SSD_GW), F32)
        for n in range(nb):
            h0 = h0_ref[n, g]
            xcol = jnp.broadcast_to(colx[gs, n:n + 1], (SSD_GW, SSD_STATE))
            dcol = jnp.broadcast_to(cold[gs, n:n + 1], (SSD_GW, SSD_STATE))
            st_ref[n, g] = h0 * dcol + xcol * jnp.broadcast_to(bm[n:n + 1, :], (SSD_GW, SSD_STATE))
            yoff = yoff + jnp.where(rown == n, _dot_nt(cm_b, h0.astype(BF16)), 0.0)
        yg = (yoff * dec_x[:, gs] + cbv * xdt[:, gs] + dvec_ref[:, gs] * xs[:, gs]) * _silu(z_ref[:, gs])
        o_ref[:, gs] = (yg * lax.rsqrt(jnp.mean(yg * yg, axis=-1, keepdims=True) + EPS)
                        * ng_ref[:, gs]).astype(o_ref.dtype)


def _ssd_sample_kernel_onto(prev_ref, *refs):
    del prev_ref
    _ssd_sample_kernel(*refs)


def _ssd_sample_call(proj, dtraw, buf_t, h0, layer, p, state_stack):
    ns = proj.shape[0]
    nb = SAMPLE_NB
    nt = SSD_TAPS - 1
    depth = h0.shape[0]
    hshape = (nb, SSD_GROUPS, SSD_GW, SSD_STATE)
    onto = state_stack is not None
    return pl.pallas_call(
        _ssd_sample_kernel_onto if onto else _ssd_sample_kernel,
        grid=(ns // nb,),
        in_specs=([pl.BlockSpec(memory_space=pl.ANY)] if onto else [])
        + [pl.BlockSpec((nb, W_BRANCH), lambda i: (i, COL_C_Z)),
                  pl.BlockSpec((nb, SSD_CONV_DIM), lambda i: (i, COL_XBC4096)),
                  pl.BlockSpec((nb, DT_PAD), lambda i: (i, 0)),
                  pl.BlockSpec((None, nt, nb, SSD_CONV_DIM), lambda i: (layer, 0, i, 0)),
                  pl.BlockSpec((None,) + hshape, lambda i: (layer, i, 0, 0, 0)),
                  _layer_spec((SSD_TAPS, SSD_CONV_DIM), layer),
                  _vec_spec(SSD_CONV_DIM, layer),
                  _vec_spec(DT_PAD, layer), _vec_spec(DT_PAD, layer),
                  _vec_spec(W_BRANCH, layer), _vec_spec(W_BRANCH, layer),
                  pl.BlockSpec((DT_PAD, W_BRANCH), lambda i: (0, 0))],
        out_specs=[pl.BlockSpec((nb, W_BRANCH), lambda i: (i, 0)),
                   pl.BlockSpec((None,) + hshape, lambda i: (layer, i, 0, 0, 0)),
                   pl.BlockSpec((nt, nb, SSD_CONV_DIM), lambda i: (0, i, 0))],
        out_shape=[jax.ShapeDtypeStruct((ns, W_BRANCH), BF16),
                   jax.ShapeDtypeStruct((depth, ns, SSD_GROUPS, SSD_GW, SSD_STATE), F32),
                   jax.ShapeDtypeStruct((nt, ns, SSD_CONV_DIM), F32)],
        scratch_shapes=[pltpu.VMEM((W_BRANCH, LANE), F32), pltpu.VMEM((W_BRANCH, LANE), F32)],
        input_output_aliases={0: 1} if onto else {},
        compiler_params=_cp("arbitrary"),
        name="ssd_sample",
    )(*([state_stack] if onto else []), proj, proj, dtraw, buf_t, h0, p["ssd_conv_w"], p["ssd_conv_b"],
      p["ssd_dt_bias"], p["ssd_a_log"], p["ssd_d"], p["ssd_norm_g"], p["head_expand"])


def _layer_norm(x, g, b):
    mu = jnp.mean(x, axis=-1, keepdims=True)
    xc = x - mu
    return xc * lax.rsqrt(jnp.mean(xc * xc, axis=-1, keepdims=True) + EPS) * g + b


MLP_ROWS = 2 * MLP_CHUNK


def _mlp_prompt_kernel(u_ref, v_ref, gate_ref, lg_ref, lb_ref, ws_ref, bst_ref, o_ref, vd):
    q = MLP_CHUNK
    vd[...] = _layer_norm(_gelu(v_ref[...]), lg_ref[...], lb_ref[...])
    row = lax.broadcasted_iota(jnp.int32, (q, q), 0)
    col = lax.broadcasted_iota(jnp.int32, (q, q), 1)
    tri = row >= col
    for g in range(MLP_GROUPS):
        gs = slice(g * MLP_GD, (g + 1) * MLP_GD)
        ws = jnp.where(tri, ws_ref[g], 0.0).astype(BF16)
        bias = jnp.broadcast_to(bst_ref[:, g:g + 1], (q, MLP_GD))
        for r0 in range(0, MLP_ROWS, q):
            rs = slice(r0, r0 + q)
            mixed = _dot(ws, vd[rs, gs].astype(BF16)) + bias
            o_ref[rs, gs] = (_gelu(u_ref[rs, gs]) * mixed * _silu(gate_ref[rs, gs])).astype(o_ref.dtype)


def _mlp_prompt_call(proj, n_seq, seq, layer, p):
    q = MLP_ROWS
    nck = seq // q
    row = lambda n, c: n * nck + c
    return pl.pallas_call(
        _mlp_prompt_kernel,
        grid=(n_seq, nck),
        in_specs=[pl.BlockSpec((q, W_BRANCH), lambda n, c: (row(n, c), COL_D_U)),
                  pl.BlockSpec((q, W_BRANCH), lambda n, c: (row(n, c), COL_D_V)),
                  pl.BlockSpec((q, W_BRANCH), lambda n, c: (row(n, c), COL_D_GATE)),
                  _vec_spec(W_BRANCH, layer), _vec_spec(W_BRANCH, layer),
                  _layer_spec((MLP_GROUPS, MLP_CHUNK, MLP_CHUNK), layer),
                  _layer_spec((MLP_CHUNK, LANE), layer)],
        out_specs=pl.BlockSpec((q, W_BRANCH), lambda n, c: (row(n, c), 0)),
        out_shape=jax.ShapeDtypeStruct((n_seq * seq, W_BRANCH), BF16),
        scratch_shapes=[pltpu.VMEM((q, W_BRANCH), F32)],
        compiler_params=_cp("arbitrary", "arbitrary"),
        name="mlp_prompt",
    )(proj, proj, proj, p["mlp_ln_g"], p["mlp_ln_b"], p["mlp_w_s"], p["mlp_b_s_t"])


def _mlp_sample_kernel(u_ref, v_ref, gate_ref, lg_ref, lb_ref, w0_ref, b0_ref, o_ref, vd_ref):
    vd = _layer_norm(_gelu(v_ref[...]), lg_ref[...], lb_ref[...])
    vd_ref[...] = vd
    mixed = w0_ref[...] * vd + b0_ref[...]
    o_ref[...] = (_gelu(u_ref[...]) * mixed * _silu(gate_ref[...])).astype(o_ref.dtype)


def _mlp_sample_call(proj, layer, p):
    ns = proj.shape[0]
    tb = 64
    vec = _vec_spec(W_BRANCH, layer)
    return pl.pallas_call(
        _mlp_sample_kernel,
        grid=(ns // tb,),
        in_specs=[pl.BlockSpec((tb, W_BRANCH), lambda i: (i, COL_D_U)),
                  pl.BlockSpec((tb, W_BRANCH), lambda i: (i, COL_D_V)),
                  pl.BlockSpec((tb, W_BRANCH), lambda i: (i, COL_D_GATE)),
                  vec, vec, vec, vec],
        out_specs=[pl.BlockSpec((tb, W_BRANCH), lambda i: (i, 0)),
                   pl.BlockSpec((tb, W_BRANCH), lambda i: (i, 0))],
        out_shape=[jax.ShapeDtypeStruct((ns, W_BRANCH), BF16),
                   jax.ShapeDtypeStruct((ns, W_BRANCH), F32)],
        compiler_params=_cp("arbitrary"),
        name="mlp_sample",
    )(proj, proj, proj, p["mlp_ln_g"], p["mlp_ln_b"], p["mlp_w0"], p["mlp_b0"])


def _merge_kernel(xn_ref, oa_ref, ob_ref, oc_ref, od_ref,
                  wm0_ref, wm1_ref, wm2_ref, wm3_ref, wb_ref, o_ref):
    xn = xn_ref[...]
    acc = None
    for b, (o_b, wm) in enumerate(((oa_ref, wm0_ref), (ob_ref, wm1_ref), (oc_ref, wm2_ref), (od_ref, wm3_ref))):
        gate = _sigmoid(_dot_nt(xn, wm[...]))
        term = gate * _dot(o_b[...], wb_ref[b])
        acc = term if acc is None else acc + term
    o_ref[...] = acc.astype(o_ref.dtype)


def _merge_call(xn, outs, layer, p, tm, tn):
    m = xn.shape[0]
    nj = D_MODEL // tn

    def wm_spec(b):
        return pl.BlockSpec((None, tn, D_MODEL), lambda i, j: (layer, b * nj + j, 0))

    ospec = pl.BlockSpec((tm, W_BRANCH), lambda i, j: (i, 0))
    return pl.pallas_call(
        _merge_kernel,
        grid=(m // tm, nj),
        in_specs=[pl.BlockSpec((tm, D_MODEL), lambda i, j: (i, 0)), ospec, ospec, ospec, ospec,
                  wm_spec(0), wm_spec(1), wm_spec(2), wm_spec(3),
                  pl.BlockSpec((None, N_BRANCH, W_BRANCH, tn), lambda i, j: (layer, 0, 0, j))],
        out_specs=pl.BlockSpec((tm, tn), lambda i, j: (i, j)),
        out_shape=jax.ShapeDtypeStruct((m, D_MODEL), BF16),
        compiler_params=_cp("arbitrary", "arbitrary"),
        name="merge",
    )(xn, *outs, p["wm_t"], p["wm_t"], p["wm_t"], p["wm_t"], p["w_branch"])


def _out_kernel(m_ref, w_ref, x_ref, g_ref, o_ref):
    z = _dot(m_ref[...], w_ref[...])
    inv = lax.rsqrt(jnp.mean(z * z, axis=-1, keepdims=True) + EPS)
    o_ref[...] = x_ref[...] + z * inv * g_ref[...]


def _out_call(merged, x, layer, p, tm):
    m = x.shape[0]
    return pl.pallas_call(
        _out_kernel,
        grid=(m // tm,),
        in_specs=[pl.BlockSpec((tm, D_MODEL), lambda i: (i, 0)),
                  pl.BlockSpec((None, D_MODEL, D_MODEL), lambda i: (layer, 0, 0), pipeline_mode=pl.Buffered(1)),
                  pl.BlockSpec((tm, D_MODEL), lambda i: (i, 0)),
                  _vec_spec(D_MODEL, layer)],
        out_specs=pl.BlockSpec((tm, D_MODEL), lambda i: (i, 0)),
        out_shape=jax.ShapeDtypeStruct((m, D_MODEL), F32),
        compiler_params=_cp("arbitrary", vmem=VMEM_LIMIT_RESIDENT),
        name="out_proj",
    )(merged, p["w_out"], x, p["norm_post_g"])


def _block_diag(mats):
    gb, c = mats.shape[-3], mats.shape[-1]
    lead = [(0, 0)] * (mats.ndim - 2)
    rows = [jnp.pad(mats[..., g, :, :], lead + [(g * c, (gb - 1 - g) * c)]) for g in range(gb)]
    return jnp.concatenate(rows, axis=-2)


def _prep_params(w):
    depth = w["w_in"].shape[0]
    p = {}
    p["w_t"] = jnp.swapaxes(w["w_in"], 1, 2)
    p["wm_t"] = p["w_t"][:, WM_START:].astype(BF16)
    p["w_branch"] = w["w_branch"].astype(BF16)
    p["w_out"] = w["w_out"].astype(BF16)
    p["s5_w_glu"] = w["s5_w_glu"].astype(BF16)
    for k in ("norm_pre_g", "norm_post_g"):
        p[k] = w[k].reshape(depth, 1, D_MODEL)
    for k in ("conv_a_b", "ln_a_g", "ln_a_b", "s5_b_glu", "ssd_norm_g", "mlp_ln_g", "mlp_ln_b"):
        p[k] = w[k].reshape(depth, 1, W_BRANCH)
    p["conv_a_w"] = w["conv_a_w"]

    rows = depth * S5_GROUPS * S5_STATE
    prep = _s5_prep_call(w["s5_a_re"].reshape(rows, 1), w["s5_a_im"].reshape(rows, 1),
                         jnp.repeat(w["s5_log_step"].reshape(-1), S5_STATE).reshape(rows, 1),
                         w["s5_b_re"].reshape(rows, S5_GROUP), w["s5_b_im"].reshape(rows, S5_GROUP))
    p["ab_re"] = prep[0].reshape(depth, 1, S5_LANES)
    p["ab_im"] = prep[1].reshape(depth, 1, S5_LANES)
    blk = lambda t, r, c: _block_diag(t.reshape(depth, S5_NB, S5_GB, r, c))
    p["wb_re"] = jnp.concatenate([blk(prep[2 + 2 * k], S5_STATE, S5_GROUP) for k in range(S5_FOLD)], axis=3).astype(BF16)
    p["wb_im"] = jnp.concatenate([blk(prep[3 + 2 * k], S5_STATE, S5_GROUP) for k in range(S5_FOLD)], axis=3).astype(BF16)
    p["wc_re"] = blk(w["s5_c_re"], S5_GROUP, S5_STATE).astype(BF16)
    p["wc_im"] = blk(w["s5_c_im"], S5_GROUP, S5_STATE).astype(BF16)
    p["s5_d"] = w["s5_d"].reshape(depth, 1, W_BRANCH)

    p["ssd_conv_w"] = w["ssd_conv_w"]
    p["ssd_conv_b"] = w["ssd_conv_b"].reshape(depth, 1, SSD_CONV_DIM)
    padh = lambda t: jnp.pad(t.reshape(depth, 1, SSD_HEADS), ((0, 0), (0, 0), (0, DT_PAD - SSD_HEADS)))
    p["ssd_dt_bias"] = padh(w["ssd_dt_bias"])
    p["ssd_a_log"] = padh(w["ssd_a_log"])
    p["ssd_d"] = jnp.repeat(w["ssd_d"], W_BRANCH // SSD_HEADS, axis=1).reshape(depth, 1, W_BRANCH)
    r = jnp.arange(DT_PAD)[:, None]
    c = jnp.arange(W_BRANCH)[None, :] // (W_BRANCH // SSD_HEADS)
    p["head_expand"] = (r == c).astype(BF16)

    p["mlp_w_s"] = w["mlp_w_s"]
    p["mlp_b_s_t"] = jnp.pad(jnp.swapaxes(w["mlp_b_s"], 1, 2), ((0, 0), (0, 0), (0, LANE - MLP_GROUPS)))
    p["mlp_w0"] = jnp.repeat(w["mlp_w_s"][:, :, 0, 0], MLP_GD, axis=1).reshape(depth, 1, W_BRANCH)
    p["mlp_b0"] = jnp.repeat(w["mlp_b_s"][:, :, 0], MLP_GD, axis=1).reshape(depth, 1, W_BRANCH)
    return p


def _dense_head(x, layer, p, tm_norm, tm, tn):
    xn, dtraw = _norm_call(x, p["norm_pre_g"], p["w_t"], layer, tm_norm)
    proj_a = _inproj_call(xn, p["w_t"], layer, 0, WA_BRANCH_COLS, tm, tn)
    proj_d = _inproj_call(xn, p["w_t"], layer, WD_START, WD_BRANCH_COLS, tm, tn)
    return xn, dtraw, proj_a, proj_d


def _dense_tail(x, xn, outs, layer, p, tm_merge, tn_merge, tm_out):
    merged = _merge_call(xn, outs, layer, p, tm_merge, tn_merge)
    return _out_call(merged, x, layer, p, tm_out)


def _prompt_layer(x, n_seq, seq, layer, p):
    xn, dtraw, proj_a, proj_d = _dense_head(x, layer, p, 512, 1024, 512)
    out_a, conv_a = _conva_prompt_call(proj_a, n_seq, seq, layer, p)
    y_b, s_re, s_im = _s5_prompt_call(proj_a, n_seq, seq, layer, p)
    out_b = _s5_glu_call(y_b, proj_a, layer, p, 512)
    out_c, ssd_st, conv_ssd = _ssd_prompt_call(proj_a, dtraw, n_seq, seq, layer, p)
    out_d = _mlp_prompt_call(proj_d, n_seq, seq, layer, p)
    y = _dense_tail(x, xn, (out_a, out_b, out_c, out_d), layer, p, 512, 256, 256)
    return y, conv_a, s_re, s_im, ssd_st, conv_ssd


def _sample_layer(x, cache_a_t, h_re, h_im, h_ssd, buf_ssd_t, layer, p, ssd_stack):
    ns = x.shape[0]
    xn, dtraw, proj_a, proj_d = _dense_head(x, layer, p, ns, ns, 512)
    out_a, conv_a_t = _conva_sample_call(proj_a, cache_a_t, layer, p)
    y_b, s_re, s_im = _s5_sample_call(proj_a, h_re, h_im, layer, p)
    out_b = _s5_glu_call(y_b, proj_a, layer, p, ns)
    out_c, ssd_st, conv_ssd_t = _ssd_sample_call(proj_a, dtraw, buf_ssd_t, h_ssd, layer, p, ssd_stack)
    out_d, v_d = _mlp_sample_call(proj_d, layer, p)
    y = _dense_tail(x, xn, (out_a, out_b, out_c, out_d), layer, p, ns, 256, ns)
    return y, conv_a_t, s_re, s_im, ssd_st, conv_ssd_t, v_d


def kernel(x_prompt, x_sample, cache_conv_a, state_s5_re, state_s5_im, state_ssd, cache_conv_ssd, norm_pre_g, w_in, conv_a_w, conv_a_b, ln_a_g, ln_a_b, s5_a_re, s5_a_im, s5_log_step, s5_b_re, s5_b_im, s5_c_re, s5_c_im, s5_d, s5_w_glu, s5_b_glu, ssd_conv_w, ssd_conv_b, ssd_dt_bias, ssd_a_log, ssd_d, ssd_norm_g, mlp_ln_g, mlp_ln_b, mlp_w_s, mlp_b_s, w_branch, w_out, norm_post_g):
    w = dict(norm_pre_g=norm_pre_g, w_in=w_in, conv_a_w=conv_a_w, conv_a_b=conv_a_b, ln_a_g=ln_a_g,
             ln_a_b=ln_a_b, s5_a_re=s5_a_re, s5_a_im=s5_a_im, s5_log_step=s5_log_step, s5_b_re=s5_b_re,
             s5_b_im=s5_b_im, s5_c_re=s5_c_re, s5_c_im=s5_c_im, s5_d=s5_d, s5_w_glu=s5_w_glu,
             s5_b_glu=s5_b_glu, ssd_conv_w=ssd_conv_w, ssd_conv_b=ssd_conv_b, ssd_dt_bias=ssd_dt_bias,
             ssd_a_log=ssd_a_log, ssd_d=ssd_d, ssd_norm_g=ssd_norm_g, mlp_ln_g=mlp_ln_g, mlp_ln_b=mlp_ln_b,
             mlp_w_s=mlp_w_s, mlp_b_s=mlp_b_s, w_branch=w_branch, w_out=w_out, norm_post_g=norm_post_g)
    depth = w_in.shape[0]
    n_seq, seq, _ = x_prompt.shape
    ns = x_sample.shape[0]
    p = _prep_params(w)
    hp = x_prompt.reshape(n_seq * seq, D_MODEL)
    hs = x_sample.reshape(ns, D_MODEL)
    cache_a_t = jnp.swapaxes(cache_conv_a, 1, 2)
    buf_ssd_t = jnp.swapaxes(cache_conv_ssd, 1, 2)
    h_re = state_s5_re.reshape(depth, ns, S5_LANES)
    h_im = state_s5_im.reshape(depth, ns, S5_LANES)
    h_ssd = state_ssd.reshape(depth, ns, SSD_GROUPS, SSD_GW, SSD_STATE)
    acc = [[] for _ in range(11)]
    ssd_stack = None
    for i in range(depth):
        hp, a1, r1, m1, s1, c1 = _prompt_layer(hp, n_seq, seq, i, p)
        hs, a2, r2, m2, ssd_stack, c2, v2 = _sample_layer(hs, cache_a_t, h_re, h_im, h_ssd, buf_ssd_t, i, p,
                                                          ssd_stack)
        vals = (a1, a2,
                r1.reshape(n_seq, S5_GROUPS, S5_STATE), r2.reshape(ns, S5_GROUPS, S5_STATE),
                m1.reshape(n_seq, S5_GROUPS, S5_STATE), m2.reshape(ns, S5_GROUPS, S5_STATE),
                s1.reshape(n_seq, SSD_GROUPS, 4, 64, SSD_STATE), None,
                c1, c2, v2.reshape(ns, 1, W_BRANCH))
        for lst, v in zip(acc, vals):
            lst.append(v)
    st = [None if l[0] is None else jnp.stack(l) for l in acc]
    st[1] = jnp.swapaxes(st[1], 1, 2)
    st[7] = ssd_stack.reshape(depth, ns, SSD_GROUPS, 4, 64, SSD_STATE)
    st[9] = jnp.swapaxes(st[9], 1, 2)
    return (hp.reshape(n_seq, seq, D_MODEL), hs.reshape(ns, 1, D_MODEL)) + tuple(st)
```

```python
import jax
import jax.numpy as jnp
from jax import lax
from jax.experimental import pallas as pl
from jax.experimental.pallas import tpu as pltpu

F32 = jnp.float32
BF16 = jnp.bfloat16
EPS = 1e-6

D_MODEL = 4096
W_BRANCH = 2048
N_BRANCH = 4
LANE = 128
CONV_A_WIDTH = 31
S5_GROUPS = 128
S5_GROUP = 16
S5_STATE = 64
S5_LANES = S5_GROUPS * S5_STATE
S5_GB = 8
S5_NB = S5_GROUPS // S5_GB
S5_BW = S5_GB * S5_STATE
S5_UW = S5_GB * S5_GROUP
S5_FOLD = 4
SSD_HEADS = 32
SSD_GROUPS = 8
SSD_STATE = 128
SSD_CHUNK = 128
SSD_CONV_DIM = 4096
SSD_GW = W_BRANCH // SSD_GROUPS
SSD_B_OFF = W_BRANCH
SSD_C_OFF = W_BRANCH + SSD_GROUPS * SSD_STATE
MLP_CHUNK = 128
MLP_GROUPS = 16
MLP_GD = W_BRANCH // MLP_GROUPS
DT_PAD = LANE
WA_BRANCH_COLS = 16384
WD_START = WA_BRANCH_COLS + SSD_HEADS
WD_BRANCH_COLS = 3 * W_BRANCH
WM_START = WD_START + WD_BRANCH_COLS
COL_A_VAL, COL_A_G, COL_A_GATE, COL_B_U, COL_B_GATE, COL_C_Z = 0, 1, 2, 3, 4, 5
COL_XBC4096 = 3
COL_D_U, COL_D_V, COL_D_GATE = 0, 1, 2

VMEM_LIMIT = 56 * 1024 * 1024
VMEM_LIMIT_RESIDENT = 60 * 1024 * 1024


def _cp(*sem, vmem=VMEM_LIMIT):
    return pltpu.CompilerParams(dimension_semantics=sem, vmem_limit_bytes=vmem)


def _vec_spec(width, layer):
    return pl.BlockSpec((None, 1, width), lambda *_: (layer, 0, 0))


def _layer_spec(shape, layer):
    return pl.BlockSpec((None,) + tuple(shape), lambda *_: (layer,) + (0,) * len(shape))


def _sigmoid(x):
    return jax.nn.sigmoid(x)


def _silu(x):
    return x * jax.nn.sigmoid(x)


def _gelu(x):
    return jax.nn.gelu(x)


def _dot(a, b):
    return jnp.dot(a, b, preferred_element_type=F32)


def _dot_nt(a, b):
    return lax.dot_general(a, b, (((1,), (1,)), ((), ())), preferred_element_type=F32)


def _split3(x):
    hi = x.astype(BF16)
    r1 = x - hi.astype(F32)
    mid = r1.astype(BF16)
    lo = (r1 - mid.astype(F32)).astype(BF16)
    return hi, mid, lo


def _dot_exact_rhs(a_bf16, x):
    hi, mid, lo = _split3(x)
    return _dot(a_bf16, hi) + _dot(a_bf16, mid) + _dot(a_bf16, lo)


def _dot_exact_lhs(x, e_bf16):
    hi, mid, lo = _split3(x)
    return _dot(hi, e_bf16) + _dot(mid, e_bf16) + _dot(lo, e_bf16)


def _norm_kernel(x_ref, g_ref, wdt_ref, xn_ref, dt_ref):
    x = x_ref[...]
    y = x * lax.rsqrt(jnp.mean(x * x, axis=-1, keepdims=True) + EPS) * g_ref[...]
    xb = y.astype(BF16)
    xn_ref[...] = xb
    d = _dot_nt(xb, wdt_ref[...].astype(BF16))
    lane = lax.broadcasted_iota(jnp.int32, d.shape, 1)
    dt_ref[...] = jnp.where(lane < SSD_HEADS, d, 0.0)


def _norm_call(x, g, w_t, layer, tm):
    m = x.shape[0]
    return pl.pallas_call(
        _norm_kernel,
        grid=(m // tm,),
        in_specs=[pl.BlockSpec((tm, D_MODEL), lambda i: (i, 0)),
                  _vec_spec(D_MODEL, layer),
                  pl.BlockSpec((None, DT_PAD, D_MODEL), lambda i: (layer, WA_BRANCH_COLS // DT_PAD, 0))],
        out_specs=[pl.BlockSpec((tm, D_MODEL), lambda i: (i, 0)),
                   pl.BlockSpec((tm, DT_PAD), lambda i: (i, 0))],
        out_shape=[jax.ShapeDtypeStruct((m, D_MODEL), BF16),
                   jax.ShapeDtypeStruct((m, DT_PAD), F32)],
        compiler_params=_cp("arbitrary"),
        name="norm_dt",
    )(x, g, w_t)


W_ROW_ALIGN = 32


def _wt_spec(layer, row0, tn):
    return pl.BlockSpec((pl.Element(1), pl.Element(tn), pl.Element(D_MODEL)),
                        lambda i, j: (layer, pl.multiple_of(row0 + j * tn, W_ROW_ALIGN), 0))


def _mm_kernel(x_ref, w_ref, o_ref):
    o_ref[...] = _dot_nt(x_ref[...], w_ref[0].astype(BF16)).astype(o_ref.dtype)


def _inproj_call(xn, w_t, layer, row0, n_cols, tm, tn):
    m, k = xn.shape
    return pl.pallas_call(
        _mm_kernel,
        grid=(m // tm, n_cols // tn),
        in_specs=[pl.BlockSpec((tm, k), lambda i, j: (i, 0)),
                  _wt_spec(layer, row0, tn)],
        out_specs=pl.BlockSpec((tm, tn), lambda i, j: (i, j)),
        out_shape=jax.ShapeDtypeStruct((m, n_cols), F32),
        compiler_params=_cp("arbitrary", "arbitrary"),
        name="in_proj",
    )(xn, w_t)


CA_TC = 256
CA_HALO = 32
CA_RB = 64
W_TILES = W_BRANCH // LANE


def _ln_swish_gate(v, lg, lb, gate):
    mu = jnp.mean(v, axis=-1, keepdims=True)
    xc = v - mu
    y = xc * lax.rsqrt(jnp.mean(xc * xc, axis=-1, keepdims=True) + EPS) * lg + lb
    return _silu(y) * _silu(gate)


def _conva_prompt_kernel(av_ref, ag_ref, gate_ref, w_ref, b_ref, lg_ref, lb_ref,
                         o_ref, nc_ref, ext_ref, cv_ref):
    c = pl.program_id(1)
    tc = CA_TC
    off = CA_HALO - (CONV_A_WIDTH - 1)

    @pl.when(c == 0)
    def _():
        ext_ref[:, 0:CA_HALO, :] = jnp.zeros((W_TILES, CA_HALO, LANE), F32)

    for lt in range(W_TILES):
        ls = slice(lt * LANE, (lt + 1) * LANE)
        ext_ref[lt, CA_HALO:CA_HALO + tc, :] = av_ref[:, ls] * _sigmoid(ag_ref[:, ls])

    def conv_tile(lt, carry):
        ls = pl.ds(pl.multiple_of(lt * LANE, LANE), LANE)
        for r0 in range(0, tc, CA_RB):
            accs = [jnp.broadcast_to(b_ref[:, ls], (8, LANE))] * (CA_RB // 8)
            for k in range(CONV_A_WIDTH):
                wk = jnp.broadcast_to(w_ref[k:k + 1, ls], (8, LANE))
                accs = [a + wk * ext_ref[lt, r0 + off + k + 8 * i:r0 + off + k + 8 * i + 8, :]
                        for i, a in enumerate(accs)]
            for i, a in enumerate(accs):
                cv_ref[r0 + 8 * i:r0 + 8 * i + 8, ls] = a
        return carry

    lax.fori_loop(0, W_TILES, conv_tile, 0)

    o_ref[...] = _ln_swish_gate(cv_ref[...], lg_ref[...], lb_ref[...], gate_ref[...]).astype(o_ref.dtype)

    @pl.when(c == pl.num_programs(1) - 1)
    def _():
        for lt in range(W_TILES):
            nc_ref[:, lt * LANE:(lt + 1) * LANE] = ext_ref[lt, tc + off:tc + CA_HALO, :]

    for lt in range(W_TILES):
        ext_ref[lt, 0:CA_HALO, :] = ext_ref[lt, tc:tc + CA_HALO, :]


def _conva_prompt_call(proj, n_seq, seq, layer, p):
    nck = seq // CA_TC
    row = lambda n, c: n * nck + c
    vec = _vec_spec(W_BRANCH, layer)
    return pl.pallas_call(
        _conva_prompt_kernel,
        grid=(n_seq, nck),
        in_specs=[pl.BlockSpec((CA_TC, W_BRANCH), lambda n, c: (row(n, c), COL_A_VAL)),
                  pl.BlockSpec((CA_TC, W_BRANCH), lambda n, c: (row(n, c), COL_A_G)),
                  pl.BlockSpec((CA_TC, W_BRANCH), lambda n, c: (row(n, c), COL_A_GATE)),
                  _layer_spec((CONV_A_WIDTH, W_BRANCH), layer),
                  vec, vec, vec],
        out_specs=[pl.BlockSpec((CA_TC, W_BRANCH), lambda n, c: (row(n, c), 0)),
                   pl.BlockSpec((None, CONV_A_WIDTH - 1, W_BRANCH), lambda n, c: (n, 0, 0))],
        out_shape=[jax.ShapeDtypeStruct((n_seq * seq, W_BRANCH), BF16),
                   jax.ShapeDtypeStruct((n_seq, CONV_A_WIDTH - 1, W_BRANCH), F32)],
        scratch_shapes=[pltpu.VMEM((W_TILES, CA_HALO + CA_TC, LANE), F32),
                        pltpu.VMEM((CA_TC, W_BRANCH), F32)],
        compiler_params=_cp("arbitrary", "arbitrary"),
        name="conv_a_prompt",
    )(proj, proj, proj, p["conv_a_w"], p["conv_a_b"], p["ln_a_g"], p["ln_a_b"])


SAMPLE_NB = 8


def _conva_sample_kernel(av_ref, ag_ref, gate_ref, cache_ref, w_ref, b_ref, lg_ref, lb_ref,
                         o_ref, nc_ref):
    h = av_ref[...] * _sigmoid(ag_ref[...])
    kw = CONV_A_WIDTH - 1
    acc = b_ref[...] + w_ref[kw:kw + 1, :] * h
    for k in range(kw):
        acc = acc + w_ref[k:k + 1, :] * cache_ref[k]
    o_ref[...] = _ln_swish_gate(acc, lg_ref[...], lb_ref[...], gate_ref[...]).astype(o_ref.dtype)
    for k in range(kw - 1):
        nc_ref[k] = cache_ref[k + 1]
    nc_ref[kw - 1] = h


def _conva_sample_call(proj, cache_t, layer, p):
    ns = proj.shape[0]
    nb = SAMPLE_NB
    kw = CONV_A_WIDTH - 1
    vec = _vec_spec(W_BRANCH, layer)
    return pl.pallas_call(
        _conva_sample_kernel,
        grid=(ns // nb,),
        in_specs=[pl.BlockSpec((nb, W_BRANCH), lambda i: (i, COL_A_VAL)),
                  pl.BlockSpec((nb, W_BRANCH), lambda i: (i, COL_A_G)),
                  pl.BlockSpec((nb, W_BRANCH), lambda i: (i, COL_A_GATE)),
                  pl.BlockSpec((None, kw, nb, W_BRANCH), lambda i: (layer, 0, i, 0)),
                  _layer_spec((CONV_A_WIDTH, W_BRANCH), layer),
                  vec, vec, vec],
        out_specs=[pl.BlockSpec((nb, W_BRANCH), lambda i: (i, 0)),
                   pl.BlockSpec((kw, nb, W_BRANCH), lambda i: (0, i, 0))],
        out_shape=[jax.ShapeDtypeStruct((ns, W_BRANCH), BF16),
                   jax.ShapeDtypeStruct((kw, ns, W_BRANCH), F32)],
        compiler_params=_cp("arbitrary"),
        name="conv_a_sample",
    )(proj, proj, proj, cache_t, p["conv_a_w"], p["conv_a_b"], p["ln_a_g"], p["ln_a_b"])


def _cmul(ar, ai, br, bi):
    return ar * br - ai * bi, ar * bi + ai * br


def _s5_prep_kernel(are_ref, aim_ref, ls_ref, bre_ref, bim_ref, abre_ref, abim_ref, *bb_refs):
    lam_re = are_ref[...]
    lam_im = aim_ref[...]
    step = jnp.exp(ls_ref[...])
    mag = jnp.exp(lam_re * step)
    ab_re = mag * jnp.cos(lam_im * step)
    ab_im = mag * jnp.sin(lam_im * step)
    den = lam_re * lam_re + lam_im * lam_im
    nr = ab_re - 1.0
    coef_re = (nr * lam_re + ab_im * lam_im) / den
    coef_im = (ab_im * lam_re - nr * lam_im) / den
    abre_ref[...] = ab_re
    abim_ref[...] = ab_im
    cur_re, cur_im = _cmul(coef_re, coef_im, bre_ref[...], bim_ref[...])
    for k in range(S5_FOLD):
        bb_refs[2 * k][...] = cur_re
        bb_refs[2 * k + 1][...] = cur_im
        cur_re, cur_im = _cmul(ab_re, ab_im, cur_re, cur_im)


def _s5_prep_call(a_re, a_im, log_step, b_re, b_im):
    rows = a_re.shape[0]
    rb = 1024
    col = pl.BlockSpec((rb, 1), lambda i: (i, 0))
    mat = pl.BlockSpec((rb, S5_GROUP), lambda i: (i, 0))
    col_sh = jax.ShapeDtypeStruct((rows, 1), F32)
    mat_sh = jax.ShapeDtypeStruct((rows, S5_GROUP), F32)
    return pl.pallas_call(
        _s5_prep_kernel,
        grid=(rows // rb,),
        in_specs=[col, col, col, mat, mat],
        out_specs=[col, col] + [mat] * (2 * S5_FOLD),
        out_shape=[col_sh, col_sh] + [mat_sh] * (2 * S5_FOLD),
        compiler_params=_cp("arbitrary"),
        name="s5_prep",
    )(a_re, a_im, log_step, b_re, b_im)


S5_TC = 2048
S5_UHEAD = 8
S5_SUB = 2048


def _s5_prompt_kernel(u_ref, abr_ref, abi_ref, wbr_ref, wbi_ref, wcr_ref, wci_ref, d_ref,
                      y_ref, sre_ref, sim_ref,
                      ush, hre, him, l4r, l4i, pwr, pwi, car, cai):
    c = pl.program_id(2)
    tc = S5_TC
    bw = S5_BW

    @pl.when(c == 0)
    def _():
        row = lax.broadcasted_iota(jnp.int32, (8, bw), 0)
        a1r = jnp.broadcast_to(abr_ref[...], (8, bw))
        a1i = jnp.broadcast_to(abi_ref[...], (8, bw))
        cr, ci = a1r, a1i
        pr = jnp.zeros((8, bw), F32)
        pi = jnp.zeros((8, bw), F32)
        for r in range(8):
            pr = jnp.where(row == r, cr, pr)
            pi = jnp.where(row == r, ci, pi)
            if r + 1 == S5_FOLD:
                l4r[...] = jnp.where(row >= S5_FOLD, cr, 0.0)
                l4i[...] = jnp.where(row >= S5_FOLD, ci, 0.0)
            cr, ci = _cmul(cr, ci, a1r, a1i)
        pwr[...] = pr
        pwi[...] = pi
        car[...] = jnp.zeros(car.shape, F32)
        cai[...] = jnp.zeros(cai.shape, F32)
        ush[0:S5_UHEAD, :] = jnp.zeros((S5_UHEAD, S5_UW), F32)

    ush[S5_UHEAD:S5_UHEAD + tc, :] = u_ref[...]
    sub = S5_SUB
    row8 = lax.broadcasted_iota(jnp.int32, (sub, S5_UW), 0) & 7
    cr = car[...]
    ci = cai[...]
    for r0 in range(0, tc, sub):
        u = u_ref[r0:r0 + sub, :]
        parts = [u.astype(BF16)]
        for k in range(1, S5_FOLD):
            uk = ush[S5_UHEAD - k + r0:S5_UHEAD - k + r0 + sub, :]
            parts.append(jnp.where(row8 >= k, uk, 0.0).astype(BF16))
        lhs = jnp.concatenate(parts, axis=-1)
        hre[r0:r0 + sub, :] = _dot_nt(lhs, wbr_ref[...])
        him[r0:r0 + sub, :] = _dot_nt(lhs, wbi_ref[...])
        for t0 in range(r0, r0 + sub, 8):
            xr = hre[t0:t0 + 8, :]
            xi = him[t0:t0 + 8, :]
            tr, ti = _cmul(l4r[...], l4i[...], pltpu.roll(xr, S5_FOLD, 0), pltpu.roll(xi, S5_FOLD, 0))
            xr = xr + tr
            xi = xi + ti
            tr, ti = _cmul(pwr[...], pwi[...], cr, ci)
            xr = xr + tr
            xi = xi + ti
            hre[t0:t0 + 8, :] = xr
            him[t0:t0 + 8, :] = xi
            cr = jnp.broadcast_to(xr[7:8, :], (8, bw))
            ci = jnp.broadcast_to(xi[7:8, :], (8, bw))
        y_ref[r0:r0 + sub, :] = (_dot_nt(hre[r0:r0 + sub, :].astype(BF16), wcr_ref[...])
                                 - _dot_nt(him[r0:r0 + sub, :].astype(BF16), wci_ref[...])
                                 + d_ref[...] * u)
    car[...] = cr
    cai[...] = ci

    @pl.when(c == pl.num_programs(2) - 1)
    def _():
        sre_ref[...] = cr[0:1, :]
        sim_ref[...] = ci[0:1, :]


def _s5_prompt_call(proj, n_seq, seq, layer, p):
    nck = seq // S5_TC
    ub = W_BRANCH // S5_UW
    lane_blk = lambda w: pl.BlockSpec((None, 1, w), lambda n, j, c: (layer, 0, j))
    wspec = lambda r, cc: pl.BlockSpec((None, None, r, cc), lambda n, j, c: (layer, j, 0, 0))
    sspec = pl.BlockSpec((None, 1, S5_BW), lambda n, j, c: (n, 0, j))
    return pl.pallas_call(
        _s5_prompt_kernel,
        grid=(n_seq, S5_NB, nck),
        in_specs=[pl.BlockSpec((S5_TC, S5_UW), lambda n, j, c: (n * nck + c, COL_B_U * ub + j)),
                  lane_blk(S5_BW), lane_blk(S5_BW),
                  wspec(S5_BW, S5_FOLD * S5_UW), wspec(S5_BW, S5_FOLD * S5_UW),
                  wspec(S5_UW, S5_BW), wspec(S5_UW, S5_BW),
                  lane_blk(S5_UW)],
        out_specs=[pl.BlockSpec((S5_TC, S5_UW), lambda n, j, c: (n * nck + c, j)), sspec, sspec],
        out_shape=[jax.ShapeDtypeStruct((n_seq * seq, W_BRANCH), F32),
                   jax.ShapeDtypeStruct((n_seq, 1, S5_LANES), F32),
                   jax.ShapeDtypeStruct((n_seq, 1, S5_LANES), F32)],
        scratch_shapes=[pltpu.VMEM((S5_UHEAD + S5_TC, S5_UW), F32),
                        pltpu.VMEM((S5_TC, S5_BW), F32), pltpu.VMEM((S5_TC, S5_BW), F32)]
        + [pltpu.VMEM((8, S5_BW), F32)] * 6,
        compiler_params=_cp("arbitrary", "arbitrary", "arbitrary"),
        name="s5_prompt",
    )(proj, p["ab_re"], p["ab_im"], p["wb_re"], p["wb_im"], p["wc_re"], p["wc_im"], p["s5_d"])


def _s5_sample_kernel(u_ref, h0r_ref, h0i_ref, abr_ref, abi_ref, wbr_ref, wbi_ref, wcr_ref, wci_ref,
                      d_ref, y_ref, sre_ref, sim_ref):
    for j in range(S5_NB):
        us = slice(j * S5_UW, (j + 1) * S5_UW)
        ss = slice(j * S5_BW, (j + 1) * S5_BW)
        ub = u_ref[:, us].astype(BF16)
        ar = abr_ref[:, ss]
        ai = abi_ref[:, ss]
        h0r = h0r_ref[:, ss]
        h0i = h0i_ref[:, ss]
        nr = ar * h0r - ai * h0i + _dot_nt(ub, wbr_ref[j])
        ni = ar * h0i + ai * h0r + _dot_nt(ub, wbi_ref[j])
        sre_ref[:, ss] = nr
        sim_ref[:, ss] = ni
        y_ref[:, us] = (_dot_nt(nr.astype(BF16), wcr_ref[j]) - _dot_nt(ni.astype(BF16), wci_ref[j])
                        + d_ref[:, us] * u_ref[:, us])


def _s5_sample_call(proj, h0_re, h0_im, layer, p):
    ns = proj.shape[0]
    tb = 64
    h0spec = pl.BlockSpec((None, tb, S5_LANES), lambda i: (layer, i, 0))
    sspec = pl.BlockSpec((tb, S5_LANES), lambda i: (i, 0))
    wb = pl.BlockSpec((None, S5_NB, S5_BW, S5_UW), lambda i: (layer, 0, 0, 0))
    wc = pl.BlockSpec((None, S5_NB, S5_UW, S5_BW), lambda i: (layer, 0, 0, 0))
    return pl.pallas_call(
        _s5_sample_kernel,
        grid=(ns // tb,),
        in_specs=[pl.BlockSpec((tb, W_BRANCH), lambda i: (i, COL_B_U)), h0spec, h0spec,
                  _vec_spec(S5_LANES, layer), _vec_spec(S5_LANES, layer), wb, wb, wc, wc,
                  _vec_spec(W_BRANCH, layer)],
        out_specs=[pl.BlockSpec((tb, W_BRANCH), lambda i: (i, 0)), sspec, sspec],
        out_shape=[jax.ShapeDtypeStruct((ns, W_BRANCH), F32),
                   jax.ShapeDtypeStruct((ns, S5_LANES), F32),
                   jax.ShapeDtypeStruct((ns, S5_LANES), F32)],
        compiler_params=_cp("arbitrary"),
        name="s5_sample",
    )(proj, h0_re, h0_im, p["ab_re"], p["ab_im"], p["wb_re"], p["wb_im"], p["wc_re"], p["wc_im"], p["s5_d"])


def _s5_glu_kernel(y_ref, gate_ref, w_ref, b_ref, o_ref):
    g = _gelu(y_ref[...])
    z = _dot(g.astype(BF16), w_ref[...]) + b_ref[...]
    o_ref[...] = (g * _sigmoid(z) * _silu(gate_ref[...])).astype(o_ref.dtype)


def _s5_glu_call(y, proj, layer, p, tm):
    m = y.shape[0]
    return pl.pallas_call(
        _s5_glu_kernel,
        grid=(m // tm,),
        in_specs=[pl.BlockSpec((tm, W_BRANCH), lambda i: (i, 0)),
                  pl.BlockSpec((tm, W_BRANCH), lambda i: (i, COL_B_GATE)),
                  _layer_spec((W_BRANCH, W_BRANCH), layer),
                  _vec_spec(W_BRANCH, layer)],
        out_specs=pl.BlockSpec((tm, W_BRANCH), lambda i: (i, 0)),
        out_shape=jax.ShapeDtypeStruct((m, W_BRANCH), BF16),
        compiler_params=_cp("arbitrary"),
        name="s5_glu",
    )(y, proj, p["s5_w_glu"], p["s5_b_glu"])


SSD_EXT_HEAD = 8
SSD_TAPS = 4
XBC_TILES = SSD_CONV_DIM // LANE


def _softplus(x):
    return jnp.maximum(x, 0.0) + jnp.log(1.0 + jnp.exp(-jnp.abs(x)))


def _ssd_prompt_kernel(z_ref, xbc_ref, dtr_ref, cw_ref, cb_ref, dtb_ref, alog_ref, dvec_ref, ng_ref,
                       o_ref, st_ref, nb_ref,
                       ext, h, xa, ybuf):
    c = pl.program_id(1)
    q = SSD_CHUNK
    e0 = SSD_EXT_HEAD
    off = e0 - (SSD_TAPS - 1)

    @pl.when(c == 0)
    def _():
        ext[:, 0:e0, :] = jnp.zeros((XBC_TILES, e0, LANE), F32)
        h[...] = jnp.zeros(h.shape, F32)

    rb = 64

    def conv_tile(lt, carry):
        ls = pl.ds(pl.multiple_of(lt * LANE, LANE), LANE)
        ext[lt, e0:e0 + q, :] = xbc_ref[:, ls]
        for r0 in range(0, q, rb):
            accs = [jnp.broadcast_to(cb_ref[:, ls], (8, LANE))] * (rb // 8)
            for k in range(SSD_TAPS):
                wk = jnp.broadcast_to(cw_ref[k:k + 1, ls], (8, LANE))
                accs = [a + wk * ext[lt, r0 + off + k + 8 * i:r0 + off + k + 8 * i + 8, :]
                        for i, a in enumerate(accs)]
            for i, a in enumerate(accs):
                xa[r0 + 8 * i:r0 + 8 * i + 8, ls] = _silu(a)
        return carry

    lax.fori_loop(0, XBC_TILES, conv_tile, 0)

    @pl.when(c == pl.num_programs(1) - 1)
    def _():
        for lt in range(XBC_TILES):
            nb_ref[:, lt * LANE:(lt + 1) * LANE] = ext[lt, q + off:q + e0, :]

    for lt in range(XBC_TILES):
        ext[lt, 0:e0, :] = ext[lt, q:q + e0, :]

    dt = _softplus(dtr_ref[...] + dtb_ref[...])
    a = -jnp.exp(alog_ref[...])
    da = dt * a
    row = lax.broadcasted_iota(jnp.int32, (q, q), 0)
    col = lax.broadcasted_iota(jnp.int32, (q, q), 1)
    tri = row >= col
    lt_ones = jnp.where(tri, 1.0, 0.0).astype(BF16)
    dacs = _dot_exact_rhs(lt_ones, da)
    dcs_t = dacs.T
    dt_t = dt.T
    tot_t = jnp.broadcast_to(dcs_t[:, q - 1:q], (q, q))
    w_t = dt_t * jnp.exp(tot_t - dcs_t)
    cd_t = jnp.exp(tot_t)
    lane_lo = col < 64
    row_lo = row < 64

    def rows_of(m, r):
        return jnp.broadcast_to(m[r:r + 1, :], (q, q))

    for g in range(SSD_GROUPS):
        bm_b = xa[:, SSD_B_OFF + g * SSD_STATE:SSD_B_OFF + (g + 1) * SSD_STATE].astype(BF16)
        cm_b = xa[:, SSD_C_OFF + g * SSD_STATE:SSD_C_OFF + (g + 1) * SSD_STATE].astype(BF16)
        cb = _dot_nt(cm_b, bm_b)
        for p in range(2):
            r_lo = 4 * g + 2 * p
            cs = slice(g * SSD_GW + p * LANE, g * SSD_GW + (p + 1) * LANE)
            xs = xa[:, cs]
            ydiag = None
            eo = []
            for hh in range(2):
                r = r_lo + hh
                a_col = jnp.broadcast_to(dacs[:, r:r + 1], (q, q))
                seg = a_col - rows_of(dcs_t, r)
                lmat = jnp.where(tri, jnp.exp(seg), 0.0)
                mp = cb * lmat * rows_of(dt_t, r)
                keep = lane_lo if hh == 0 else jnp.logical_not(lane_lo)
                xs_h = jnp.where(keep, xs, 0.0)
                term = _dot(mp.astype(BF16), xs_h.astype(BF16))
                ydiag = term if ydiag is None else ydiag + term
                eo.append(jnp.exp(a_col))
            eo_pair = jnp.where(lane_lo, eo[0], eo[1])
            hs = slice(p * LANE, (p + 1) * LANE)
            h_pair = h[g, hs, :]
            yoff = _dot_nt(cm_b, h_pair.astype(BF16)) * eo_pair
            ybuf[:, cs] = ydiag + yoff + dvec_ref[:, cs] * xs
            w_rows = jnp.where(row_lo, rows_of(w_t, r_lo), rows_of(w_t, r_lo + 1))
            cd_rows = jnp.where(row_lo, rows_of(cd_t, r_lo), rows_of(cd_t, r_lo + 1))
            st = _dot((xs.T * w_rows).astype(BF16), bm_b)
            h[g, hs, :] = h_pair * cd_rows + st

    y = ybuf[...] * _silu(z_ref[...])
    for g in range(SSD_GROUPS):
        gs = slice(g * SSD_GW, (g + 1) * SSD_GW)
        yg = y[:, gs]
        o_ref[:, gs] = (yg * lax.rsqrt(jnp.mean(yg * yg, axis=-1, keepdims=True) + EPS)
                        * ng_ref[:, gs]).astype(o_ref.dtype)

    @pl.when(c == pl.num_programs(1) - 1)
    def _():
        st_ref[...] = h[...]


def _ssd_prompt_call(proj, dtraw, n_seq, seq, layer, p):
    q = SSD_CHUNK
    nck = seq // q
    row = lambda n, c: n * nck + c
    return pl.pallas_call(
        _ssd_prompt_kernel,
        grid=(n_seq, nck),
        in_specs=[pl.BlockSpec((q, W_BRANCH), lambda n, c: (row(n, c), COL_C_Z)),
                  pl.BlockSpec((q, SSD_CONV_DIM), lambda n, c: (row(n, c), COL_XBC4096)),
                  pl.BlockSpec((q, DT_PAD), lambda n, c: (row(n, c), 0)),
                  _layer_spec((SSD_TAPS, SSD_CONV_DIM), layer),
                  _vec_spec(SSD_CONV_DIM, layer),
                  _vec_spec(DT_PAD, layer), _vec_spec(DT_PAD, layer),
                  _vec_spec(W_BRANCH, layer), _vec_spec(W_BRANCH, layer)],
        out_specs=[pl.BlockSpec((q, W_BRANCH), lambda n, c: (row(n, c), 0)),
                   pl.BlockSpec((None, SSD_GROUPS, SSD_GW, SSD_STATE), lambda n, c: (n, 0, 0, 0)),
                   pl.BlockSpec((None, SSD_TAPS - 1, SSD_CONV_DIM), lambda n, c: (n, 0, 0))],
        out_shape=[jax.ShapeDtypeStruct((n_seq * seq, W_BRANCH), BF16),
                   jax.ShapeDtypeStruct((n_seq, SSD_GROUPS, SSD_GW, SSD_STATE), F32),
                   jax.ShapeDtypeStruct((n_seq, SSD_TAPS - 1, SSD_CONV_DIM), F32)],
        scratch_shapes=[pltpu.VMEM((XBC_TILES, SSD_EXT_HEAD + q, LANE), F32),
                        pltpu.VMEM((SSD_GROUPS, SSD_GW, SSD_STATE), F32),
                        pltpu.VMEM((q, SSD_CONV_DIM), F32),
                        pltpu.VMEM((q, W_BRANCH), F32)],
        compiler_params=_cp("arbitrary", "arbitrary"),
        name="ssd_prompt",
    )(proj, proj, dtraw, p["ssd_conv_w"], p["ssd_conv_b"], p["ssd_dt_bias"], p["ssd_a_log"],
      p["ssd_d"], p["ssd_norm_g"])


def _ssd_sample_kernel(z_ref, xbc_ref, dtr_ref, buf_ref, h0_ref, cw_ref, cb_ref, dtb_ref, alog_ref,
                       dvec_ref, ng_ref, ex_ref,
                       o_ref, st_ref, nb_ref, colx, cold):
    nb = SAMPLE_NB
    xbc = xbc_ref[...]
    acc = cb_ref[...] + cw_ref[SSD_TAPS - 1:SSD_TAPS, :] * xbc
    for k in range(SSD_TAPS - 1):
        acc = acc + cw_ref[k:k + 1, :] * buf_ref[k]
    xa = _silu(acc)
    for k in range(SSD_TAPS - 2):
        nb_ref[k] = buf_ref[k + 1]
    nb_ref[SSD_TAPS - 2] = xbc

    dt = _softplus(dtr_ref[...] + dtb_ref[...])
    a = -jnp.exp(alog_ref[...])
    dec = jnp.exp(dt * a)
    ex = ex_ref[...]
    dt_x = _dot_exact_lhs(dt, ex)
    dec_x = _dot_exact_lhs(dec, ex)
    xs = xa[:, 0:W_BRANCH]
    xdt = xs * dt_x

    pad = jnp.zeros((LANE - nb, W_BRANCH), F32)
    colx[...] = jnp.concatenate([xdt, pad], axis=0).T
    cold[...] = jnp.concatenate([dec_x, pad], axis=0).T

    rown = lax.broadcasted_iota(jnp.int32, (nb, SSD_GW), 0)
    for g in range(SSD_GROUPS):
        gs = slice(g * SSD_GW, (g + 1) * SSD_GW)
        bm = xa[:, SSD_B_OFF + g * SSD_STATE:SSD_B_OFF + (g + 1) * SSD_STATE]
        cm = xa[:, SSD_C_OFF + g * SSD_STATE:SSD_C_OFF + (g + 1) * SSD_STATE]
        cm_b = cm.astype(BF16)
        cbv = jnp.sum(cm * bm, axis=-1, keepdims=True)
        yoff = jnp.zeros((nb, SSD_GW), F32)
        for n in range(nb):
            h0 = h0_ref[n, g]
            xcol = jnp.broadcast_to(colx[gs, n:n + 1], (SSD_GW, SSD_STATE))
            dcol = jnp.broadcast_to(cold[gs, n:n + 1], (SSD_GW, SSD_STATE))
            st_ref[n, g] = h0 * dcol + xcol * jnp.broadcast_to(bm[n:n + 1, :], (SSD_GW, SSD_STATE))
            yoff = yoff + jnp.where(rown == n, _dot_nt(cm_b, h0.astype(BF16)), 0.0)
        yg = (yoff * dec_x[:, gs] + cbv * xdt[:, gs] + dvec_ref[:, gs] * xs[:, gs]) * _silu(z_ref[:, gs])
        o_ref[:, gs] = (yg * lax.rsqrt(jnp.mean(yg * yg, axis=-1, keepdims=True) + EPS)
                        * ng_ref[:, gs]).astype(o_ref.dtype)


def _ssd_sample_call(proj, dtraw, buf_t, h0, layer, p):
    ns = proj.shape[0]
    nb = SAMPLE_NB
    nt = SSD_TAPS - 1
    hshape = (nb, SSD_GROUPS, SSD_GW, SSD_STATE)
    return pl.pallas_call(
        _ssd_sample_kernel,
        grid=(ns // nb,),
        in_specs=[pl.BlockSpec((nb, W_BRANCH), lambda i: (i, COL_C_Z)),
                  pl.BlockSpec((nb, SSD_CONV_DIM), lambda i: (i, COL_XBC4096)),
                  pl.BlockSpec((nb, DT_PAD), lambda i: (i, 0)),
                  pl.BlockSpec((None, nt, nb, SSD_CONV_DIM), lambda i: (layer, 0, i, 0)),
                  pl.BlockSpec((None,) + hshape, lambda i: (layer, i, 0, 0, 0)),
                  _layer_spec((SSD_TAPS, SSD_CONV_DIM), layer),
                  _vec_spec(SSD_CONV_DIM, layer),
                  _vec_spec(DT_PAD, layer), _vec_spec(DT_PAD, layer),
                  _vec_spec(W_BRANCH, layer), _vec_spec(W_BRANCH, layer),
                  pl.BlockSpec((DT_PAD, W_BRANCH), lambda i: (0, 0))],
        out_specs=[pl.BlockSpec((nb, W_BRANCH), lambda i: (i, 0)),
                   pl.BlockSpec(hshape, lambda i: (i, 0, 0, 0)),
                   pl.BlockSpec((nt, nb, SSD_CONV_DIM), lambda i: (0, i, 0))],
        out_shape=[jax.ShapeDtypeStruct((ns, W_BRANCH), BF16),
                   jax.ShapeDtypeStruct((ns, SSD_GROUPS, SSD_GW, SSD_STATE), F32),
                   jax.ShapeDtypeStruct((nt, ns, SSD_CONV_DIM), F32)],
        scratch_shapes=[pltpu.VMEM((W_BRANCH, LANE), F32), pltpu.VMEM((W_BRANCH, LANE), F32)],
        compiler_params=_cp("arbitrary"),
        name="ssd_sample",
    )(proj, proj, dtraw, buf_t, h0, p["ssd_conv_w"], p["ssd_conv_b"],
      p["ssd_dt_bias"], p["ssd_a_log"], p["ssd_d"], p["ssd_norm_g"], p["head_expand"])


def _layer_norm(x, g, b):
    mu = jnp.mean(x, axis=-1, keepdims=True)
    xc = x - mu
    return xc * lax.rsqrt(jnp.mean(xc * xc, axis=-1, keepdims=True) + EPS) * g + b


MLP_ROWS = 2 * MLP_CHUNK


def _mlp_prompt_kernel(u_ref, v_ref, gate_ref, lg_ref, lb_ref, ws_ref, bst_ref, o_ref, vd):
    q = MLP_CHUNK
    vd[...] = _layer_norm(_gelu(v_ref[...]), lg_ref[...], lb_ref[...])
    row = lax.broadcasted_iota(jnp.int32, (q, q), 0)
    col = lax.broadcasted_iota(jnp.int32, (q, q), 1)
    tri = row >= col
    for g in range(MLP_GROUPS):
        gs = slice(g * MLP_GD, (g + 1) * MLP_GD)
        ws = jnp.where(tri, ws_ref[g], 0.0).astype(BF16)
        bias = jnp.broadcast_to(bst_ref[:, g:g + 1], (q, MLP_GD))
        for r0 in range(0, MLP_ROWS, q):
            rs = slice(r0, r0 + q)
            mixed = _dot(ws, vd[rs, gs].astype(BF16)) + bias
            o_ref[rs, gs] = (_gelu(u_ref[rs, gs]) * mixed * _silu(gate_ref[rs, gs])).astype(o_ref.dtype)


def _mlp_prompt_call(proj, n_seq, seq, layer, p):
    q = MLP_ROWS
    nck = seq // q
    row = lambda n, c: n * nck + c
    return pl.pallas_call(
        _mlp_prompt_kernel,
        grid=(n_seq, nck),
        in_specs=[pl.BlockSpec((q, W_BRANCH), lambda n, c: (row(n, c), COL_D_U)),
                  pl.BlockSpec((q, W_BRANCH), lambda n, c: (row(n, c), COL_D_V)),
                  pl.BlockSpec((q, W_BRANCH), lambda n, c: (row(n, c), COL_D_GATE)),
                  _vec_spec(W_BRANCH, layer), _vec_spec(W_BRANCH, layer),
                  _layer_spec((MLP_GROUPS, MLP_CHUNK, MLP_CHUNK), layer),
                  _layer_spec((MLP_CHUNK, LANE), layer)],
        out_specs=pl.BlockSpec((q, W_BRANCH), lambda n, c: (row(n, c), 0)),
        out_shape=jax.ShapeDtypeStruct((n_seq * seq, W_BRANCH), BF16),
        scratch_shapes=[pltpu.VMEM((q, W_BRANCH), F32)],
        compiler_params=_cp("arbitrary", "arbitrary"),
        name="mlp_prompt",
    )(proj, proj, proj, p["mlp_ln_g"], p["mlp_ln_b"], p["mlp_w_s"], p["mlp_b_s_t"])


def _mlp_sample_kernel(u_ref, v_ref, gate_ref, lg_ref, lb_ref, w0_ref, b0_ref, o_ref, vd_ref):
    vd = _layer_norm(_gelu(v_ref[...]), lg_ref[...], lb_ref[...])
    vd_ref[...] = vd
    mixed = w0_ref[...] * vd + b0_ref[...]
    o_ref[...] = (_gelu(u_ref[...]) * mixed * _silu(gate_ref[...])).astype(o_ref.dtype)


def _mlp_sample_call(proj, layer, p):
    ns = proj.shape[0]
    tb = 64
    vec = _vec_spec(W_BRANCH, layer)
    return pl.pallas_call(
        _mlp_sample_kernel,
        grid=(ns // tb,),
        in_specs=[pl.BlockSpec((tb, W_BRANCH), lambda i: (i, COL_D_U)),
                  pl.BlockSpec((tb, W_BRANCH), lambda i: (i, COL_D_V)),
                  pl.BlockSpec((tb, W_BRANCH), lambda i: (i, COL_D_GATE)),
                  vec, vec, vec, vec],
        out_specs=[pl.BlockSpec((tb, W_BRANCH), lambda i: (i, 0)),
                   pl.BlockSpec((tb, W_BRANCH), lambda i: (i, 0))],
        out_shape=[jax.ShapeDtypeStruct((ns, W_BRANCH), BF16),
                   jax.ShapeDtypeStruct((ns, W_BRANCH), F32)],
        compiler_params=_cp("arbitrary"),
        name="mlp_sample",
    )(proj, proj, proj, p["mlp_ln_g"], p["mlp_ln_b"], p["mlp_w0"], p["mlp_b0"])


def _merge_kernel(xn_ref, oa_ref, ob_ref, oc_ref, od_ref,
                  wm0_ref, wm1_ref, wm2_ref, wm3_ref, wb_ref, o_ref):
    xn = xn_ref[...]
    acc = None
    for b, (o_b, wm) in enumerate(((oa_ref, wm0_ref), (ob_ref, wm1_ref), (oc_ref, wm2_ref), (od_ref, wm3_ref))):
        gate = _sigmoid(_dot_nt(xn, wm[...]))
        term = gate * _dot(o_b[...], wb_ref[b])
        acc = term if acc is None else acc + term
    o_ref[...] = acc.astype(o_ref.dtype)


def _merge_call(xn, outs, layer, p, tm, tn):
    m = xn.shape[0]
    nj = D_MODEL // tn

    def wm_spec(b):
        return pl.BlockSpec((None, tn, D_MODEL), lambda i, j: (layer, b * nj + j, 0))

    ospec = pl.BlockSpec((tm, W_BRANCH), lambda i, j: (i, 0))
    return pl.pallas_call(
        _merge_kernel,
        grid=(m // tm, nj),
        in_specs=[pl.BlockSpec((tm, D_MODEL), lambda i, j: (i, 0)), ospec, ospec, ospec, ospec,
                  wm_spec(0), wm_spec(1), wm_spec(2), wm_spec(3),
                  pl.BlockSpec((None, N_BRANCH, W_BRANCH, tn), lambda i, j: (layer, 0, 0, j))],
        out_specs=pl.BlockSpec((tm, tn), lambda i, j: (i, j)),
        out_shape=jax.ShapeDtypeStruct((m, D_MODEL), BF16),
        compiler_params=_cp("arbitrary", "arbitrary"),
        name="merge",
    )(xn, *outs, p["wm_t"], p["wm_t"], p["wm_t"], p["wm_t"], p["w_branch"])


def _out_kernel(m_ref, w_ref, x_ref, g_ref, o_ref):
    z = _dot(m_ref[...], w_ref[...])
    inv = lax.rsqrt(jnp.mean(z * z, axis=-1, keepdims=True) + EPS)
    o_ref[...] = x_ref[...] + z * inv * g_ref[...]


def _out_call(merged, x, layer, p, tm):
    m = x.shape[0]
    return pl.pallas_call(
        _out_kernel,
        grid=(m // tm,),
        in_specs=[pl.BlockSpec((tm, D_MODEL), lambda i: (i, 0)),
                  pl.BlockSpec((None, D_MODEL, D_MODEL), lambda i: (layer, 0, 0), pipeline_mode=pl.Buffered(1)),
                  pl.BlockSpec((tm, D_MODEL), lambda i: (i, 0)),
                  _vec_spec(D_MODEL, layer)],
        out_specs=pl.BlockSpec((tm, D_MODEL), lambda i: (i, 0)),
        out_shape=jax.ShapeDtypeStruct((m, D_MODEL), F32),
        compiler_params=_cp("arbitrary", vmem=VMEM_LIMIT_RESIDENT),
        name="out_proj",
    )(merged, p["w_out"], x, p["norm_post_g"])


def _block_diag(mats):
    gb, c = mats.shape[-3], mats.shape[-1]
    lead = [(0, 0)] * (mats.ndim - 2)
    rows = [jnp.pad(mats[..., g, :, :], lead + [(g * c, (gb - 1 - g) * c)]) for g in range(gb)]
    return jnp.concatenate(rows, axis=-2)


def _prep_params(w):
    depth = w["w_in"].shape[0]
    p = {}
    p["w_t"] = jnp.swapaxes(w["w_in"], 1, 2)
    p["wm_t"] = p["w_t"][:, WM_START:].astype(BF16)
    p["w_branch"] = w["w_branch"].astype(BF16)
    p["w_out"] = w["w_out"].astype(BF16)
    p["s5_w_glu"] = w["s5_w_glu"].astype(BF16)
    for k in ("norm_pre_g", "norm_post_g"):
        p[k] = w[k].reshape(depth, 1, D_MODEL)
    for k in ("conv_a_b", "ln_a_g", "ln_a_b", "s5_b_glu", "ssd_norm_g", "mlp_ln_g", "mlp_ln_b"):
        p[k] = w[k].reshape(depth, 1, W_BRANCH)
    p["conv_a_w"] = w["conv_a_w"]

    rows = depth * S5_GROUPS * S5_STATE
    prep = _s5_prep_call(w["s5_a_re"].reshape(rows, 1), w["s5_a_im"].reshape(rows, 1),
                         jnp.repeat(w["s5_log_step"].reshape(-1), S5_STATE).reshape(rows, 1),
                         w["s5_b_re"].reshape(rows, S5_GROUP), w["s5_b_im"].reshape(rows, S5_GROUP))
    p["ab_re"] = prep[0].reshape(depth, 1, S5_LANES)
    p["ab_im"] = prep[1].reshape(depth, 1, S5_LANES)
    blk = lambda t, r, c: _block_diag(t.reshape(depth, S5_NB, S5_GB, r, c))
    p["wb_re"] = jnp.concatenate([blk(prep[2 + 2 * k], S5_STATE, S5_GROUP) for k in range(S5_FOLD)], axis=3).astype(BF16)
    p["wb_im"] = jnp.concatenate([blk(prep[3 + 2 * k], S5_STATE, S5_GROUP) for k in range(S5_FOLD)], axis=3).astype(BF16)
    p["wc_re"] = blk(w["s5_c_re"], S5_GROUP, S5_STATE).astype(BF16)
    p["wc_im"] = blk(w["s5_c_im"], S5_GROUP, S5_STATE).astype(BF16)
    p["s5_d"] = w["s5_d"].reshape(depth, 1, W_BRANCH)

    p["ssd_conv_w"] = w["ssd_conv_w"]
    p["ssd_conv_b"] = w["ssd_conv_b"].reshape(depth, 1, SSD_CONV_DIM)
    padh = lambda t: jnp.pad(t.reshape(depth, 1, SSD_HEADS), ((0, 0), (0, 0), (0, DT_PAD - SSD_HEADS)))
    p["ssd_dt_bias"] = padh(w["ssd_dt_bias"])
    p["ssd_a_log"] = padh(w["ssd_a_log"])
    p["ssd_d"] = jnp.repeat(w["ssd_d"], W_BRANCH // SSD_HEADS, axis=1).reshape(depth, 1, W_BRANCH)
    r = jnp.arange(DT_PAD)[:, None]
    c = jnp.arange(W_BRANCH)[None, :] // (W_BRANCH // SSD_HEADS)
    p["head_expand"] = (r == c).astype(BF16)

    p["mlp_w_s"] = w["mlp_w_s"]
    p["mlp_b_s_t"] = jnp.pad(jnp.swapaxes(w["mlp_b_s"], 1, 2), ((0, 0), (0, 0), (0, LANE - MLP_GROUPS)))
    p["mlp_w0"] = jnp.repeat(w["mlp_w_s"][:, :, 0, 0], MLP_GD, axis=1).reshape(depth, 1, W_BRANCH)
    p["mlp_b0"] = jnp.repeat(w["mlp_b_s"][:, :, 0], MLP_GD, axis=1).reshape(depth, 1, W_BRANCH)
    return p


def _dense_head(x, layer, p, tm_norm, tm, tn):
    xn, dtraw = _norm_call(x, p["norm_pre_g"], p["w_t"], layer, tm_norm)
    proj_a = _inproj_call(xn, p["w_t"], layer, 0, WA_BRANCH_COLS, tm, tn)
    proj_d = _inproj_call(xn, p["w_t"], layer, WD_START, WD_BRANCH_COLS, tm, tn)
    return xn, dtraw, proj_a, proj_d


def _dense_tail(x, xn, outs, layer, p, tm_merge, tn_merge, tm_out):
    merged = _merge_call(xn, outs, layer, p, tm_merge, tn_merge)
    return _out_call(merged, x, layer, p, tm_out)


def _prompt_layer(x, n_seq, seq, layer, p):
    xn, dtraw, proj_a, proj_d = _dense_head(x, layer, p, 512, 1024, 512)
    out_a, conv_a = _conva_prompt_call(proj_a, n_seq, seq, layer, p)
    y_b, s_re, s_im = _s5_prompt_call(proj_a, n_seq, seq, layer, p)
    out_b = _s5_glu_call(y_b, proj_a, layer, p, 512)
    out_c, ssd_st, conv_ssd = _ssd_prompt_call(proj_a, dtraw, n_seq, seq, layer, p)
    out_d = _mlp_prompt_call(proj_d, n_seq, seq, layer, p)
    y = _dense_tail(x, xn, (out_a, out_b, out_c, out_d), layer, p, 512, 256, 256)
    return y, conv_a, s_re, s_im, ssd_st, conv_ssd


def _sample_layer(x, cache_a_t, h_re, h_im, h_ssd, buf_ssd_t, layer, p):
    ns = x.shape[0]
    xn, dtraw, proj_a, proj_d = _dense_head(x, layer, p, ns, ns, 512)
    out_a, conv_a_t = _conva_sample_call(proj_a, cache_a_t, layer, p)
    y_b, s_re, s_im = _s5_sample_call(proj_a, h_re, h_im, layer, p)
    out_b = _s5_glu_call(y_b, proj_a, layer, p, ns)
    out_c, ssd_st, conv_ssd_t = _ssd_sample_call(proj_a, dtraw, buf_ssd_t, h_ssd, layer, p)
    out_d, v_d = _mlp_sample_call(proj_d, layer, p)
    y = _dense_tail(x, xn, (out_a, out_b, out_c, out_d), layer, p, ns, 256, ns)
    return y, conv_a_t, s_re, s_im, ssd_st, conv_ssd_t, v_d


def kernel(x_prompt, x_sample, cache_conv_a, state_s5_re, state_s5_im, state_ssd, cache_conv_ssd, norm_pre_g, w_in, conv_a_w, conv_a_b, ln_a_g, ln_a_b, s5_a_re, s5_a_im, s5_log_step, s5_b_re, s5_b_im, s5_c_re, s5_c_im, s5_d, s5_w_glu, s5_b_glu, ssd_conv_w, ssd_conv_b, ssd_dt_bias, ssd_a_log, ssd_d, ssd_norm_g, mlp_ln_g, mlp_ln_b, mlp_w_s, mlp_b_s, w_branch, w_out, norm_post_g):
    w = dict(norm_pre_g=norm_pre_g, w_in=w_in, conv_a_w=conv_a_w, conv_a_b=conv_a_b, ln_a_g=ln_a_g,
             ln_a_b=ln_a_b, s5_a_re=s5_a_re, s5_a_im=s5_a_im, s5_log_step=s5_log_step, s5_b_re=s5_b_re,
             s5_b_im=s5_b_im, s5_c_re=s5_c_re, s5_c_im=s5_c_im, s5_d=s5_d, s5_w_glu=s5_w_glu,
             s5_b_glu=s5_b_glu, ssd_conv_w=ssd_conv_w, ssd_conv_b=ssd_conv_b, ssd_dt_bias=ssd_dt_bias,
             ssd_a_log=ssd_a_log, ssd_d=ssd_d, ssd_norm_g=ssd_norm_g, mlp_ln_g=mlp_ln_g, mlp_ln_b=mlp_ln_b,
             mlp_w_s=mlp_w_s, mlp_b_s=mlp_b_s, w_branch=w_branch, w_out=w_out, norm_post_g=norm_post_g)
    depth = w_in.shape[0]
    n_seq, seq, _ = x_prompt.shape
    ns = x_sample.shape[0]
    p = _prep_params(w)
    hp = x_prompt.reshape(n_seq * seq, D_MODEL)
    hs = x_sample.reshape(ns, D_MODEL)
    cache_a_t = jnp.swapaxes(cache_conv_a, 1, 2)
    buf_ssd_t = jnp.swapaxes(cache_conv_ssd, 1, 2)
    h_re = state_s5_re.reshape(depth, ns, S5_LANES)
    h_im = state_s5_im.reshape(depth, ns, S5_LANES)
    h_ssd = state_ssd.reshape(depth, ns, SSD_GROUPS, SSD_GW, SSD_STATE)
    acc = [[] for _ in range(11)]
    for i in range(depth):
        hp, a1, r1, m1, s1, c1 = _prompt_layer(hp, n_seq, seq, i, p)
        hs, a2, r2, m2, s2, c2, v2 = _sample_layer(hs, cache_a_t, h_re, h_im, h_ssd, buf_ssd_t, i, p)
        vals = (a1, a2,
                r1.reshape(n_seq, S5_GROUPS, S5_STATE), r2.reshape(ns, S5_GROUPS, S5_STATE),
                m1.reshape(n_seq, S5_GROUPS, S5_STATE), m2.reshape(ns, S5_GROUPS, S5_STATE),
                s1.reshape(n_seq, SSD_GROUPS, 4, 64, SSD_STATE), s2.reshape(ns, SSD_GROUPS, 4, 64, SSD_STATE),
                c1, c2, v2.reshape(ns, 1, W_BRANCH))
        for lst, v in zip(acc, vals):
            lst.append(v)
    st = [jnp.stack(l) for l in acc]
    st[1] = jnp.swapaxes(st[1], 1, 2)
    st[9] = jnp.swapaxes(st[9], 1, 2)
    return (hp.reshape(n_seq, seq, D_MODEL), hs.reshape(ns, 1, D_MODEL)) + tuple(st)
```
